```python
import jax
import jax.numpy as jnp
from jax import lax
import numpy as np

D_MODEL = 1024
BATCH = 16
SEQ = 256
DEPTH = 2
DEC_BATCH = 8
DEC_SEQ = 4096
PAST_LEN = 256

GRID_W = 64
ROPE_BASE = 10000.0
NORM_EPS = 1e-6
QBLOCK = 128
NEG_BIG = -1e30
MLA_HEADS = 8
MLA_NOPE = 64
MLA_ROPE = 32
MLA_QK = MLA_NOPE + MLA_ROPE
MLA_V = 64
MLA_Q_RANK = 256
MLA_KV_RANK = 128
MLA_SCALE = MLA_QK ** -0.5
ML_HEADS = 4
ML_DK = 64
ML_DV = 128
ML_CHUNK = 64
SW_HEADS = 8
SW_KV_HEADS = 2
SW_HD = 64
SW_WINDOW = 128
SW_BLOCK = 128
SW_SCALE = SW_HD ** -0.5
RW_HEADS = 8
RW_N = 64
RW_DIM = RW_HEADS * RW_N
RW_W_RANK = 64
RW_A_RANK = 64
RW_G_RANK = 128
RW_DECAY_SCALE = 0.6065306597126334
RW_GN_EPS = 64e-5
D_FF = 4 * D_MODEL
N_BRANCH = 4
N_MOD = 6
IN_WIDTHS = (
    MLA_Q_RANK, MLA_KV_RANK, MLA_ROPE,
    ML_HEADS * ML_DK, ML_HEADS * ML_DK, ML_HEADS * ML_DV,
    2 * ML_HEADS, 2 * ML_HEADS, ML_HEADS * ML_DV,
    SW_HEADS * SW_HD, SW_KV_HEADS * SW_HD, SW_KV_HEADS * SW_HD,
    RW_DIM, RW_DIM, RW_DIM, 2 * RW_W_RANK, 2 * RW_A_RANK, RW_G_RANK,
    N_BRANCH * D_MODEL,
)
D_IN = sum(IN_WIDTHS)

kernel_name = 'hybrid_diffusion_prefix_trunk_step'


def rmsnorm(x, g):
    xf = x.astype(jnp.float32)
    y = xf * lax.rsqrt(jnp.mean(xf * xf, -1, keepdims=True) + NORM_EPS)
    return (y * g.astype(jnp.float32)).astype(x.dtype)


def split_columns(u):
    offsets = []
    acc = 0
    for w_ in IN_WIDTHS[:-1]:
        acc += w_
        offsets.append(acc)
    return jnp.split(u, offsets, axis=-1)


def grid_positions(n_tokens):
    t = jnp.arange(n_tokens, dtype=jnp.int32)
    return (t // GRID_W).astype(jnp.float32), (t % GRID_W).astype(jnp.float32)


def axial_rope(x, pos_r, pos_c):
    R = x.shape[-1]
    q = R // 4
    inv = ROPE_BASE ** (-jnp.arange(q, dtype=jnp.float32) / q)
    ar = pos_r[:, None] * inv
    ac = pos_c[:, None] * inv
    ang = jnp.concatenate([ar, ar, ac, ac], -1)[:, None, :]
    xf = x.astype(jnp.float32)
    rot = jnp.concatenate([-xf[..., q:2 * q], xf[..., :q], -xf[..., 3 * q:], xf[..., 2 * q:3 * q]], -1)
    return (xf * jnp.cos(ang) + rot * jnp.sin(ang)).astype(x.dtype)


def dense_attention(q, k, v, sink, scale):
    B, Sq, H, dq = q.shape
    G = k.shape[2]
    rep = H // G
    dv = v.shape[-1]
    nb = Sq // QBLOCK
    qb = jnp.moveaxis(q.reshape(B, nb, QBLOCK, G, rep, dq), 1, 0)

    def one(qblk):
        s = jnp.einsum('bqgrd,bkgd->bgrqk', qblk, k).astype(jnp.float32) * scale
        if sink is not None:
            s_snk = jnp.broadcast_to(sink.astype(jnp.float32).reshape(1, G, rep, 1, 1), s.shape[:-1] + (1,))
            p = jax.nn.softmax(jnp.concatenate([s, s_snk], -1), -1)[..., :-1]
        else:
            p = jax.nn.softmax(s, -1)
        return jnp.einsum('bgrqk,bkgd->bqgrd', p.astype(v.dtype), v)

    o = lax.map(one, qb)
    return jnp.moveaxis(o, 0, 1).reshape(B, Sq, H, dv)


def window_attention(q, k, v, k_ctx, v_ctx, sink, scale):
    B, S, H, d = q.shape
    G = k.shape[2]
    rep = H // G
    C = k_ctx.shape[1]
    nb = S // SW_BLOCK
    pad = ((0, 0), (SW_BLOCK, SW_BLOCK), (0, 0), (0, 0))
    kp = jnp.pad(k, pad).reshape(B, nb + 2, SW_BLOCK, G, d)
    vp = jnp.pad(v, pad).reshape(B, nb + 2, SW_BLOCK, G, d)
    kwin = jnp.concatenate([kp[:, :-2], kp[:, 1:-1], kp[:, 2:]], axis=2)
    vwin = jnp.concatenate([vp[:, :-2], vp[:, 1:-1], vp[:, 2:]], axis=2)
    qi = jnp.arange(SW_BLOCK)
    kj = jnp.arange(3 * SW_BLOCK)
    rel = kj[None, :] - SW_BLOCK - qi[:, None]
    kpos = jnp.arange(nb)[:, None] * SW_BLOCK - SW_BLOCK + kj[None, :]
    mask = (jnp.abs(rel) <= SW_WINDOW)[None] & ((kpos >= 0) & (kpos < S))[:, None, :]
    qb = q.reshape(B, nb, SW_BLOCK, G, rep, d)
    sink_gr = sink.astype(jnp.float32).reshape(1, G, rep, 1, 1)
    n_loc = 3 * SW_BLOCK

    def one(args):
        qblk, kblk, vblk, mblk = args
        s_loc = jnp.einsum('bqgrd,bkgd->bgrqk', qblk, kblk).astype(jnp.float32) * scale
        s_loc = jnp.where(mblk, s_loc, NEG_BIG)
        s_ctx = jnp.einsum('bqgrd,bcgd->bgrqc', qblk, k_ctx).astype(jnp.float32) * scale
        s_snk = jnp.broadcast_to(sink_gr, s_loc.shape[:-1] + (1,))
        p = jax.nn.softmax(jnp.concatenate([s_loc, s_ctx, s_snk], -1), -1).astype(v.dtype)
        return (jnp.einsum('bgrqk,bkgd->bqgrd', p[..., :n_loc], vblk)
                + jnp.einsum('bgrqc,bcgd->bqgrd', p[..., n_loc:n_loc + C], v_ctx))

    o = lax.map(one, (jnp.moveaxis(qb, 1, 0), jnp.moveaxis(kwin, 1, 0), jnp.moveaxis(vwin, 1, 0), mask))
    return jnp.moveaxis(o, 0, 1).reshape(B, S, H, d)


def mla_keys_values(c_kv, k_rope, w_ukv, k_norm):
    B, S, _ = c_kv.shape
    kv = (c_kv @ w_ukv).reshape(B, S, MLA_HEADS, MLA_NOPE + MLA_V)
    k_nope, v = kv[..., :MLA_NOPE], kv[..., MLA_NOPE:]
    k = jnp.concatenate([k_nope, jnp.broadcast_to(k_rope[:, :, None, :], (B, S, MLA_HEADS, MLA_ROPE))], -1)
    return rmsnorm(k, k_norm), v


def rope_tail(x, pos):
    return jnp.concatenate([x[..., :MLA_NOPE], axial_rope(x[..., MLA_NOPE:], *pos)], -1)


def mlstm_scan(q, k, v, li, lf, C0, n0, m0):
    B, S, H, _ = q.shape
    L = ML_CHUNK
    nc = S // L

    def to_chunks(a):
        a = a.reshape((B, nc, L, H) + a.shape[3:])
        return jnp.moveaxis(a, (1, 3), (0, 2))

    causal = jnp.tril(jnp.ones((L, L), dtype=bool))

    def body(carry, inp):
        C, n, m = carry
        qc, kc, vc, ic, fc = inp
        b = jnp.cumsum(fc, -1)
        D = jnp.where(causal, b[..., :, None] - b[..., None, :] + ic[..., None, :], -jnp.inf)
        inter = b + m[..., None]
        m_t = jnp.maximum(inter, jnp.max(D, -1))
        W = jnp.einsum('bhtd,bhsd->bhts', qc, kc) * jnp.exp(D - m_t[..., None])
        a_in = jnp.exp(inter - m_t)
        num = jnp.einsum('bhts,bhsv->bhtv', W, vc) + a_in[..., None] * jnp.einsum('bhtd,bhdv->bhtv', qc, C)
        den = jnp.sum(W, -1) + a_in * jnp.einsum('bhtd,bhd->bht', qc, n)
        h = num / jnp.maximum(jnp.abs(den), jnp.exp(-m_t))[..., None]
        bL = b[..., -1]
        g = bL[..., None] - b + ic
        m_new = jnp.maximum(bL + m, jnp.max(g, -1))
        wk = jnp.exp(g - m_new[..., None])
        decay = jnp.exp(bL + m - m_new)
        C_new = decay[..., None, None] * C + jnp.einsum('bhs,bhsd,bhsv->bhdv', wk, kc, vc)
        n_new = decay[..., None] * n + jnp.einsum('bhs,bhsd->bhd', wk, kc)
        return (C_new, n_new, m_new), h

    (C, n, m), h = lax.scan(body, (C0, n0, m0), tuple(to_chunks(t) for t in (q, k, v, li, lf)))
    h = jnp.moveaxis(h, (0, 2), (1, 3)).reshape(B, S, H, v.shape[-1])
    return h, C, n, m


def rwkv7_scan(r, w, kt, v, kh, a, S0):
    xs = tuple(jnp.moveaxis(t, 1, 0) for t in (r, w, kt, v, kh, a))

    def step(Sm, inp):
        r_t, w_t, k_t, v_t, kh_t, a_t = inp
        sk = jnp.einsum('bhvk,bhk->bhv', Sm, kh_t)
        Sn = Sm * w_t[:, :, None, :] - sk[..., None] * (a_t * kh_t)[:, :, None, :] + v_t[..., None] * k_t[:, :, None, :]
        return Sn, jnp.einsum('bhvk,bhk->bhv', Sn, r_t)

    Sf, y = lax.scan(step, S0, xs)
    return jnp.moveaxis(y, 0, 1), Sf


def rwkv_mixer(r_r, r_k, r_v, r_w1, r_a1, r_g1, P, l, S0):
    B, S, _ = r_r.shape
    f32 = jnp.float32
    hv = lambda t: t.astype(f32).reshape(B, S, RW_HEADS, RW_N)
    r, k, v = hv(r_r), hv(r_k), hv(r_v)
    kappa = k * P['rwkv_kk'][l].astype(f32).reshape(RW_HEADS, RW_N)
    kh = kappa * lax.rsqrt(jnp.sum(kappa * kappa, -1, keepdims=True) + 1e-12)
    g = (jax.nn.sigmoid(r_g1) @ P['rwkv_g2'][l]).astype(f32)
    ka = P['rwkv_ka'][l].astype(f32).reshape(RW_HEADS, RW_N)
    ys, bonuses, states = [], [], []
    for d in range(2):
        w_pre = jnp.tanh(r_w1[..., d * RW_W_RANK:(d + 1) * RW_W_RANK]) @ P['rwkv_w2'][l, d] + P['rwkv_w0'][l, d]
        w = jnp.exp(-RW_DECAY_SCALE * jax.nn.sigmoid(w_pre.astype(f32))).reshape(B, S, RW_HEADS, RW_N)
        a_pre = r_a1[..., d * RW_A_RANK:(d + 1) * RW_A_RANK] @ P['rwkv_a2'][l, d] + P['rwkv_a0'][l, d]
        a = jax.nn.sigmoid(a_pre.astype(f32)).reshape(B, S, RW_HEADS, RW_N)
        kt = k * (1.0 + (a - 1.0) * ka)
        seqs = (r, w, kt, v, kh, a)
        if d == 1:
            seqs = tuple(jnp.flip(t, 1) for t in seqs)
        y_d, S_d = rwkv7_scan(*seqs, S0[:, d].astype(f32))
        if d == 1:
            y_d = jnp.flip(y_d, 1)
        ys.append(y_d)
        bonuses.append(jnp.sum(r * kt * P['rwkv_u'][l, d].astype(f32).reshape(RW_HEADS, RW_N), -1, keepdims=True) * v)
        states.append(S_d)
    y = ys[0] + ys[1]
    mu = jnp.mean(y, -1, keepdims=True)
    var = jnp.mean(jnp.square(y - mu), -1, keepdims=True)
    yn = (y - mu) * lax.rsqrt(var + RW_GN_EPS)
    yn = yn * P['rwkv_gn_g'][l].astype(f32).reshape(RW_HEADS, RW_N) + P['rwkv_gn_b'][l].astype(f32).reshape(RW_HEADS, RW_N)
    out = ((yn + bonuses[0] + bonuses[1]).reshape(B, S, RW_DIM) * g).astype(r_r.dtype)
    return out, jnp.stack(states, 1)


def token_mixers(h, P, l, ctx, pos):
    B, S, _ = h.shape
    f32 = jnp.float32
    (q_a, kv_a, k_rope, m_q, m_k, m_v, m_i, m_f, m_o,
     s_q, s_k, s_v, r_r, r_k, r_v, r_w1, r_a1, r_g1, gate_pre) = split_columns(h @ P['w_in'][l])

    c_kv = rmsnorm(kv_a, P['mla_kv_a_norm'][l])
    q_lat = rmsnorm(q_a, P['mla_q_a_norm'][l])
    q_mla = rmsnorm((q_lat @ P['mla_w_uq'][l]).reshape(B, S, MLA_HEADS, MLA_QK), P['mla_q_norm'][l])
    k_mla, v_mla = mla_keys_values(c_kv, k_rope, P['mla_w_ukv'][l], P['mla_k_norm'][l])
    if ctx is None:
        y_a = dense_attention(q_mla, k_mla, v_mla, None, MLA_SCALE)
    else:
        q_mla = rope_tail(q_mla, pos)
        k_mla = rope_tail(k_mla, pos)
        k_c, v_c = mla_keys_values(ctx['mla_ckv'], ctx['mla_krope'], P['mla_w_ukv'][l], P['mla_k_norm'][l])
        y_a = dense_attention(q_mla, jnp.concatenate([k_mla, k_c], 1), jnp.concatenate([v_mla, v_c], 1), None, MLA_SCALE)
    y_a = y_a.reshape(B, S, MLA_HEADS * MLA_V)

    mq = m_q.astype(f32).reshape(B, S, ML_HEADS, ML_DK) * (ML_DK ** -0.5)
    mk = m_k.astype(f32).reshape(B, S, ML_HEADS, ML_DK)
    mv = m_v.astype(f32).reshape(B, S, ML_HEADS, ML_DV)
    li = m_i.astype(f32).reshape(B, S, 2, ML_HEADS) + P['mlstm_i_bias'][l].astype(f32)
    lf = jax.nn.log_sigmoid(m_f.astype(f32).reshape(B, S, 2, ML_HEADS) + P['mlstm_f_bias'][l].astype(f32))
    if ctx is None:
        C0 = jnp.zeros((B, 2, ML_HEADS, ML_DK, ML_DV), f32)
        n0 = jnp.zeros((B, 2, ML_HEADS, ML_DK), f32)
        m0 = jnp.zeros((B, 2, ML_HEADS), f32)
    else:
        C0 = ctx['mlstm_C'].astype(f32)
        n0 = ctx['mlstm_n'].astype(f32)
        m0 = ctx['mlstm_m'].astype(f32)
    h_f, C_f, n_f, m_f_ = mlstm_scan(mq, mk, mv, li[:, :, 0], lf[:, :, 0], C0[:, 0], n0[:, 0], m0[:, 0])
    fl = lambda t: jnp.flip(t, 1)
    h_b, C_b, n_b, m_b = mlstm_scan(fl(mq), fl(mk), fl(mv), fl(li[:, :, 1]), fl(lf[:, :, 1]), C0[:, 1], n0[:, 1], m0[:, 1])
    h_ml = rmsnorm(h_f + fl(h_b), P['mlstm_norm'][l]) * jax.nn.sigmoid(m_o.astype(f32)).reshape(B, S, ML_HEADS, ML_DV)
    y_b = h_ml.reshape(B, S, ML_HEADS * ML_DV).astype(h.dtype)

    sq = rmsnorm(s_q.reshape(B, S, SW_HEADS, SW_HD), P['swa_q_norm'][l])
    sk = rmsnorm(s_k.reshape(B, S, SW_KV_HEADS, SW_HD), P['swa_k_norm'][l])
    sv = s_v.reshape(B, S, SW_KV_HEADS, SW_HD)
    if ctx is None:
        y_c = dense_attention(sq, sk, sv, P['swa_sink'][l], SW_SCALE)
    else:
        y_c = window_attention(axial_rope(sq, *pos), axial_rope(sk, *pos), sv,
                               ctx['swa_k'], ctx['swa_v'], P['swa_sink'][l], SW_SCALE)
    y_c = y_c.reshape(B, S, SW_HEADS * SW_HD)

    rw0 = None if ctx is None else ctx['rwkv']
    if rw0 is None:
        rw0 = jnp.zeros((B, 2, RW_HEADS, RW_N, RW_N), f32)
    y_d, S_rw = rwkv_mixer(r_r, r_k, r_v, r_w1, r_a1, r_g1, P, l, rw0)

    gates = jax.nn.sigmoid(gate_pre.astype(f32)).astype(h.dtype).reshape(B, S, N_BRANCH, D_MODEL)
    merged = (gates[:, :, 0] * (y_a @ P['mla_w_o'][l]) + gates[:, :, 1] * (y_b @ P['mlstm_w_o'][l])
              + gates[:, :, 2] * (y_c @ P['swa_w_o'][l]) + gates[:, :, 3] * (y_d @ P['rwkv_w_o'][l]))
    out = merged @ P['w_out'][l]
    new_ctx = {'mla_ckv': c_kv, 'mla_krope': k_rope, 'swa_k': sk, 'swa_v': sv,
               'mlstm_C': jnp.stack([C_f, C_b], 1), 'mlstm_n': jnp.stack([n_f, n_b], 1),
               'mlstm_m': jnp.stack([m_f_, m_b], 1), 'rwkv': S_rw}
    return out, new_ctx


def trunk_layer(x, mod, P, l, ctx, pos):
    shift1, scale1, gate1, shift2, scale2, gate2 = jnp.split(mod.astype(x.dtype), N_MOD, -1)
    h = rmsnorm(x, P['norm1'][l]) * (1.0 + scale1) + shift1
    mix, new_ctx = token_mixers(h, P, l, ctx, pos)
    x = x + gate1 * mix
    h = rmsnorm(x, P['norm2'][l]) * (1.0 + scale2) + shift2
    f = jnp.square(jax.nn.relu(h @ P['mlp_w1'][l])) @ P['mlp_w2'][l]
    return x + gate2 * f, new_ctx


def setup_inputs(seed: int = 0) -> dict:
    key = jax.random.key(seed)
    ks = iter(jax.random.split(key, 64))
    L = DEPTH

    def nrm(shape, scale=1.0):
        return jax.random.normal(next(ks), shape, jnp.float32) * scale

    def gain(shape):
        return 1.0 + nrm(shape, 0.02)

    return {
        'x_prompt': nrm((BATCH, SEQ, D_MODEL)),
        'x_sample': nrm((DEC_BATCH, DEC_SEQ, D_MODEL)),
        'c': nrm((DEC_BATCH, D_MODEL)),
        'cache_mla_ckv': nrm((DEC_BATCH, L, PAST_LEN, MLA_KV_RANK)),
        'cache_mla_krope': nrm((DEC_BATCH, L, PAST_LEN, MLA_ROPE)),
        'cache_swa_k': nrm((DEC_BATCH, L, PAST_LEN, SW_KV_HEADS, SW_HD)),
        'cache_swa_v': nrm((DEC_BATCH, L, PAST_LEN, SW_KV_HEADS, SW_HD)),
        'state_mlstm_C': nrm((DEC_BATCH, L, 2, ML_HEADS, ML_DK, ML_DV), 0.1),
        'state_mlstm_n': nrm((DEC_BATCH, L, 2, ML_HEADS, ML_DK), 0.1),
        'state_mlstm_m': nrm((DEC_BATCH, L, 2, ML_HEADS), 0.5),
        'state_rwkv': nrm((DEC_BATCH, L, 2, RW_HEADS, RW_N, RW_N), 0.1),
        'c_ctx': nrm((D_MODEL,)),
        'ada_w': nrm((L, D_MODEL, N_MOD * D_MODEL), 0.5 * D_MODEL ** -0.5),
        'ada_b': nrm((L, N_MOD * D_MODEL), 0.02),
        'norm1': gain((L, D_MODEL)),
        'norm2': gain((L, D_MODEL)),
        'w_in': nrm((L, D_MODEL, D_IN), D_MODEL ** -0.5),
        'mla_q_a_norm': gain((L, MLA_Q_RANK)),
        'mla_kv_a_norm': gain((L, MLA_KV_RANK)),
        'mla_w_uq': nrm((L, MLA_Q_RANK, MLA_HEADS * MLA_QK), MLA_Q_RANK ** -0.5),
        'mla_w_ukv': nrm((L, MLA_KV_RANK, MLA_HEADS * (MLA_NOPE + MLA_V)), MLA_KV_RANK ** -0.5),
        'mla_q_norm': gain((L, MLA_QK)),
        'mla_k_norm': gain((L, MLA_QK)),
        'mla_w_o': nrm((L, MLA_HEADS * MLA_V, D_MODEL), (MLA_HEADS * MLA_V) ** -0.5),
        'mlstm_i_bias': nrm((L, 2, ML_HEADS), 0.1),
        'mlstm_f_bias': jnp.linspace(3.0, 6.0, ML_HEADS)[None, None, :] + nrm((L, 2, ML_HEADS), 0.1),
        'mlstm_norm': gain((L, ML_DV)),
        'mlstm_w_o': nrm((L, ML_HEADS * ML_DV, D_MODEL), (ML_HEADS * ML_DV) ** -0.5),
        'swa_q_norm': gain((L, SW_HD)),
        'swa_k_norm': gain((L, SW_HD)),
        'swa_sink': nrm((L, SW_HEADS), 0.5),
        'swa_w_o': nrm((L, SW_HEADS * SW_HD, D_MODEL), (SW_HEADS * SW_HD) ** -0.5),
        'rwkv_w0': nrm((L, 2, RW_DIM), 0.5),
        'rwkv_w2': nrm((L, 2, RW_W_RANK, RW_DIM), 0.5 * RW_W_RANK ** -0.5),
        'rwkv_a0': nrm((L, 2, RW_DIM), 0.1),
        'rwkv_a2': nrm((L, 2, RW_A_RANK, RW_DIM), 0.5 * RW_A_RANK ** -0.5),
        'rwkv_g2': nrm((L, RW_G_RANK, RW_DIM), RW_G_RANK ** -0.5),
        'rwkv_kk': 0.85 + nrm((L, RW_DIM), 0.05),
        'rwkv_ka': 1.0 + nrm((L, RW_DIM), 0.05),
        'rwkv_u': nrm((L, 2, RW_DIM), 0.3),
        'rwkv_gn_g': gain((L, RW_DIM)),
        'rwkv_gn_b': nrm((L, RW_DIM), 0.02),
        'rwkv_w_o': nrm((L, RW_DIM, D_MODEL), RW_DIM ** -0.5),
        'w_out': nrm((L, D_MODEL, D_MODEL), D_MODEL ** -0.5),
        'mlp_w1': nrm((L, D_MODEL, D_FF), D_MODEL ** -0.5),
        'mlp_w2': nrm((L, D_FF, D_MODEL), D_FF ** -0.5),
    }


def reference(x_prompt, x_sample, c, cache_mla_ckv, cache_mla_krope, cache_swa_k, cache_swa_v,
              state_mlstm_C, state_mlstm_n, state_mlstm_m, state_rwkv,
              c_ctx, ada_w, ada_b, norm1, norm2, w_in,
              mla_q_a_norm, mla_kv_a_norm, mla_w_uq, mla_w_ukv, mla_q_norm, mla_k_norm, mla_w_o,
              mlstm_i_bias, mlstm_f_bias, mlstm_norm, mlstm_w_o,
              swa_q_norm, swa_k_norm, swa_sink, swa_w_o,
              rwkv_w0, rwkv_w2, rwkv_a0, rwkv_a2, rwkv_g2, rwkv_kk, rwkv_ka, rwkv_u, rwkv_gn_g, rwkv_gn_b, rwkv_w_o,
              w_out, mlp_w1, mlp_w2):
    P = {'norm1': norm1, 'norm2': norm2, 'w_in': w_in,
         'mla_q_a_norm': mla_q_a_norm, 'mla_kv_a_norm': mla_kv_a_norm, 'mla_w_uq': mla_w_uq,
         'mla_w_ukv': mla_w_ukv, 'mla_q_norm': mla_q_norm, 'mla_k_norm': mla_k_norm, 'mla_w_o': mla_w_o,
         'mlstm_i_bias': mlstm_i_bias, 'mlstm_f_bias': mlstm_f_bias, 'mlstm_norm': mlstm_norm, 'mlstm_w_o': mlstm_w_o,
         'swa_q_norm': swa_q_norm, 'swa_k_norm': swa_k_norm, 'swa_sink': swa_sink, 'swa_w_o': swa_w_o,
         'rwkv_w0': rwkv_w0, 'rwkv_w2': rwkv_w2, 'rwkv_a0': rwkv_a0, 'rwkv_a2': rwkv_a2, 'rwkv_g2': rwkv_g2,
         'rwkv_kk': rwkv_kk, 'rwkv_ka': rwkv_ka, 'rwkv_u': rwkv_u, 'rwkv_gn_g': rwkv_gn_g, 'rwkv_gn_b': rwkv_gn_b,
         'rwkv_w_o': rwkv_w_o, 'w_out': w_out, 'mlp_w1': mlp_w1, 'mlp_w2': mlp_w2}

    x = x_prompt
    states = []
    for l in range(DEPTH):
        mod = (jax.nn.silu(c_ctx) @ ada_w[l] + ada_b[l])[None, None, :]
        x, st = trunk_layer(x, mod, P, l, None, None)
        states.append(st)
    y_prompt = x
    new_mla_ckv = jnp.stack([s['mla_ckv'] for s in states], 1)
    new_mla_krope = jnp.stack([s['mla_krope'] for s in states], 1)
    new_swa_k = jnp.stack([s['swa_k'] for s in states], 1)
    new_swa_v = jnp.stack([s['swa_v'] for s in states], 1)
    new_mlstm_C = jnp.stack([s['mlstm_C'] for s in states], 1)
    new_mlstm_n = jnp.stack([s['mlstm_n'] for s in states], 1)
    new_mlstm_m = jnp.stack([s['mlstm_m'] for s in states], 1)
    new_rwkv = jnp.stack([s['rwkv'] for s in states], 1)

    pos = grid_positions(x_sample.shape[1])
    x = x_sample
    for l in range(DEPTH):
        mod = (jax.nn.silu(c) @ ada_w[l] + ada_b[l])[:, None, :]
        ctx = {'mla_ckv': cache_mla_ckv[:, l], 'mla_krope': cache_mla_krope[:, l],
               'swa_k': cache_swa_k[:, l], 'swa_v': cache_swa_v[:, l],
               'mlstm_C': state_mlstm_C[:, l], 'mlstm_n': state_mlstm_n[:, l], 'mlstm_m': state_mlstm_m[:, l],
               'rwkv': state_rwkv[:, l]}
        x, _ = trunk_layer(x, mod, P, l, ctx, pos)
    y_sample = x
    return (y_prompt, y_sample, new_mla_ckv, new_mla_krope, new_swa_k, new_swa_v,
            new_mlstm_C, new_mlstm_n, new_mlstm_m, new_rwkv)
```

```python
import functools

import jax
import jax.numpy as jnp
import numpy as np
from jax import lax
from jax.experimental import pallas as pl
from jax.experimental.pallas import tpu as pltpu

F32 = jnp.float32
BF16 = jnp.bfloat16

LANES = 128
VMEM_LIMIT_BYTES = 56 * 1024 * 1024

D_MODEL = 1024
NORM_EPS = 1e-6
ROPE_BASE = 10000.0
GRID_W = 64
NEG_BIG = -1e30
N_MOD = 6
N_BRANCH = 4
D_FF = 4 * D_MODEL

MLA_HEADS, MLA_NOPE, MLA_ROPE, MLA_V = 8, 64, 32, 64
MLA_QK = MLA_NOPE + MLA_ROPE
MLA_Q_RANK, MLA_KV_RANK = 256, 128
MLA_SCALE = MLA_QK ** -0.5

ML_HEADS, ML_DK, ML_DV, ML_CHUNK = 4, 64, 128, 64

SW_HEADS, SW_KV_HEADS, SW_HD = 8, 2, 64
SW_REP = SW_HEADS // SW_KV_HEADS
SW_BLOCK = 128
SW_SCALE = SW_HD ** -0.5

RW_HEADS, RW_N = 8, 64
RW_DIM = RW_HEADS * RW_N
RW_W_RANK, RW_A_RANK, RW_G_RANK = 64, 64, 128
RW_DECAY_SCALE = 0.6065306597126334
RW_GN_EPS = 64e-5
RW_CHUNK = 64

_IN_WIDTHS = (
    MLA_Q_RANK, MLA_KV_RANK, MLA_ROPE,
    ML_HEADS * ML_DK, ML_HEADS * ML_DK, ML_HEADS * ML_DV,
    2 * ML_HEADS, 2 * ML_HEADS, ML_HEADS * ML_DV,
    SW_HEADS * SW_HD, SW_KV_HEADS * SW_HD, SW_KV_HEADS * SW_HD,
    RW_DIM, RW_DIM, RW_DIM, 2 * RW_W_RANK, 2 * RW_A_RANK, RW_G_RANK,
    N_BRANCH * D_MODEL,
)
_IN_OFFS = tuple(int(v) for v in np.concatenate([[0], np.cumsum(_IN_WIDTHS)]))

WA = 512
WB = 4 * 512 + 2 * LANES
WC = 1024 + 256 + 256
WD = 3 * 1024 + 256 + 256 + 128 + 128
RW_W1_BLK = 3 * 1024 // LANES
RW_A1_BLK = RW_W1_BLK + 2


def _cparams(sem):
    return pltpu.CompilerParams(dimension_semantics=sem, vmem_limit_bytes=VMEM_LIMIT_BYTES)


def _dot(a, b):
    return jnp.dot(a, b, preferred_element_type=F32)


def _dot_nt(a, b):
    return lax.dot_general(a, b, (((1,), (1,)), ((), ())), preferred_element_type=F32)


def _dot_tn(a, b):
    return lax.dot_general(a, b, (((0,), (0,)), ((), ())), preferred_element_type=F32)


def _split3(x):
    h = x.astype(BF16)
    r = x - h.astype(F32)
    m = r.astype(BF16)
    l = (r - m.astype(F32)).astype(BF16)
    return h, m, l


def _mask_dot(mask_bf16, x):
    h, m, l = _split3(x)
    return _dot(mask_bf16, h) + _dot(mask_bf16, m) + _dot(mask_bf16, l)


def _dot_mask(x, mask_bf16):
    h, m, l = _split3(x)
    return _dot(h, mask_bf16) + _dot(m, mask_bf16) + _dot(l, mask_bf16)


def _sigmoid(x):
    return 1.0 / (1.0 + jnp.exp(-x))


def _log_sigmoid(x):
    return jnp.minimum(x, 0.0) - jnp.log1p(jnp.exp(-jnp.abs(x)))


def _mm_kernel(x_ref, w_ref, b_ref, o_ref, *, pre, act):
    x = x_ref[...]
    if pre == "silu":
        xf = x.astype(F32)
        x = (xf * _sigmoid(xf)).astype(BF16)
    acc = _dot(x, w_ref[...]) + b_ref[...]
    if act == "sigmoid":
        acc = _sigmoid(acc)
    o_ref[...] = acc.astype(o_ref.dtype)


def _mm(x, w, bias=None, *, tm, tn, pre=None, act=None, out_dtype=F32, name="mm"):
    M, K = x.shape
    N = w.shape[1]
    assert M % tm == 0 and N % tn == 0, (M, tm, N, tn)
    if bias is None:
        bias = jnp.zeros((1, N), F32)
    return pl.pallas_call(
        functools.partial(_mm_kernel, pre=pre, act=act),
        grid=(M // tm, N // tn),
        in_specs=[pl.BlockSpec((tm, K), lambda i, j: (i, 0)),
                  pl.BlockSpec((K, tn), lambda i, j: (0, j)),
                  pl.BlockSpec((1, tn), lambda i, j: (0, j))],
        out_specs=pl.BlockSpec((tm, tn), lambda i, j: (i, j)),
        out_shape=jax.ShapeDtypeStruct((M, N), out_dtype),
        compiler_params=_cparams(("parallel", "parallel")),
        name=name,
    )(x, w, bias)


def _norm_mod_kernel(x_ref, g_ref, shift_ref, scale_ref, o_ref):
    x = x_ref[...]
    y = x * lax.rsqrt(jnp.mean(x * x, axis=-1, keepdims=True) + NORM_EPS)
    o_ref[...] = ((y * g_ref[...]) * (1.0 + scale_ref[...]) + shift_ref[...]).astype(o_ref.dtype)


def _norm_mod(x, g, mod, shift_idx, scale_idx, *, ts):
    Bm, Sm, D = x.shape
    return pl.pallas_call(
        _norm_mod_kernel,
        grid=(Bm, Sm // ts),
        in_specs=[pl.BlockSpec((None, ts, D), lambda b, s: (b, s, 0)),
                  pl.BlockSpec((1, D), lambda b, s: (0, 0)),
                  pl.BlockSpec((None, 1, D), lambda b, s: (b, 0, shift_idx)),
                  pl.BlockSpec((None, 1, D), lambda b, s: (b, 0, scale_idx))],
        out_specs=pl.BlockSpec((None, ts, D), lambda b, s: (b, s, 0)),
        out_shape=jax.ShapeDtypeStruct((Bm, Sm, D), BF16),
        compiler_params=_cparams(("parallel", "parallel")),
        name="norm_mod",
    )(x, g, mod, mod)


def _mlp_kernel(h_ref, x_ref, gate_ref, w1_ref, w2_ref, o_ref, acc_ref):
    j = pl.program_id(2)
    a = jnp.maximum(_dot(h_ref[...], w1_ref[...]), 0.0)
    part = _dot((a * a).astype(BF16), w2_ref[...])

    @pl.when(j == 0)
    def _():
        acc_ref[...] = part

    @pl.when(j > 0)
    def _():
        acc_ref[...] += part

    @pl.when(j == pl.num_programs(2) - 1)
    def _():
        o_ref[...] = x_ref[...] + gate_ref[...] * acc_ref[...]


def _mlp(h, x, mod, w1, w2, *, ts, tf):
    Bm, Sm, D = x.shape
    FF = w1.shape[1]
    return pl.pallas_call(
        _mlp_kernel,
        grid=(Bm, Sm // ts, FF // tf),
        in_specs=[pl.BlockSpec((None, ts, D), lambda b, s, j: (b, s, 0)),
                  pl.BlockSpec((None, ts, D), lambda b, s, j: (b, s, 0)),
                  pl.BlockSpec((None, 1, D), lambda b, s, j: (b, 0, 5)),
                  pl.BlockSpec((D, tf), lambda b, s, j: (0, j)),
                  pl.BlockSpec((tf, D), lambda b, s, j: (j, 0))],
        out_specs=pl.BlockSpec((None, ts, D), lambda b, s, j: (b, s, 0)),
        out_shape=jax.ShapeDtypeStruct((Bm, Sm, D), F32),
        scratch_shapes=[pltpu.VMEM((ts, D), F32)],
        compiler_params=_cparams(("parallel", "parallel", "arbitrary")),
        name="mlp",
    )(h, x, mod, w1, w2)


def _merge_kernel(h_ref, ya_ref, yb_ref, yc_ref, yd_ref, x_ref, gate_ref, wg_ref, wo_ref, wout_ref,
                  o_ref, acc_ref):
    i = pl.program_id(2)
    g = _sigmoid(_dot(h_ref[...], wg_ref[...]))
    for idx, y_ref in enumerate((ya_ref, yb_ref, yc_ref, yd_ref)):
        @pl.when(i == idx)
        def _(idx=idx, y_ref=y_ref):
            contrib = g * _dot(y_ref[...], wo_ref[0:y_ref.shape[-1], :])
            if idx == 0:
                acc_ref[...] = contrib
            else:
                acc_ref[...] += contrib

    @pl.when(i == N_BRANCH - 1)
    def _():
        o_ref[...] = x_ref[...] + gate_ref[...] * _dot(acc_ref[...].astype(BF16), wout_ref[...])


def _merge(h, ya, yb, yc, yd, x, mod, wg, wo, wout, *, ts):
    Bm, Sm, D = x.shape

    def tok(width):
        return pl.BlockSpec((None, ts, width), lambda b, s, i: (b, s, 0))

    return pl.pallas_call(
        _merge_kernel,
        grid=(Bm, Sm // ts, N_BRANCH),
        in_specs=[tok(D), tok(ya.shape[-1]), tok(yb.shape[-1]), tok(yc.shape[-1]), tok(yd.shape[-1]), tok(D),
                  pl.BlockSpec((None, 1, D), lambda b, s, i: (b, 0, 2)),
                  pl.BlockSpec((None, D, D), lambda b, s, i: (i, 0, 0)),
                  pl.BlockSpec((None, D, D), lambda b, s, i: (i, 0, 0)),
                  pl.BlockSpec((D, D), lambda b, s, i: (0, 0))],
        out_specs=tok(D),
        out_shape=jax.ShapeDtypeStruct((Bm, Sm, D), F32),
        scratch_shapes=[pltpu.VMEM((ts, D), F32)],
        compiler_params=_cparams(("parallel", "parallel", "arbitrary")),
        name="merge",
    )(h, ya, yb, yc, yd, x, mod, wg, wo, wout)


def _rope_tables(n_tokens, start, rot):
    q = rot // 4
    t = jnp.arange(n_tokens, dtype=jnp.int32)
    pos_r = (t // GRID_W).astype(F32)
    pos_c = (t % GRID_W).astype(F32)
    inv = ROPE_BASE ** (-jnp.arange(q, dtype=F32) / q)
    ar = pos_r[:, None] * inv
    ac = pos_c[:, None] * inv
    zeros = jnp.zeros_like(ar)
    cos_t = jnp.concatenate([jnp.cos(ar), jnp.cos(ar), jnp.cos(ac), jnp.cos(ac)], -1)
    sin_prev = jnp.concatenate([zeros, jnp.sin(ar), zeros, jnp.sin(ac)], -1)
    sin_next = jnp.concatenate([-jnp.sin(ar), zeros, -jnp.sin(ac), zeros], -1)

    def place(tab, fill):
        left = jnp.full((n_tokens, start), fill, F32)
        right = jnp.full((n_tokens, LANES - start - rot), fill, F32)
        return jnp.concatenate([left, tab, right], -1)

    return place(cos_t, 1.0), place(sin_prev, 0.0), place(sin_next, 0.0)


def _rope(x, cos_t, sin_prev, sin_next, q):
    return (x * cos_t + pltpu.roll(x, q, 1) * sin_prev + pltpu.roll(x, LANES - q, 1) * sin_next)


def _mla_q_kernel(u_ref, qan_ref, wq_ref, qn_ref, cos_ref, sp_ref, sn_ref, q_ref, qlat_ref, *, rope):
    h = pl.program_id(2)

    @pl.when(h == 0)
    def _():
        qa = u_ref[...]
        y = qa * lax.rsqrt(jnp.mean(qa * qa, axis=-1, keepdims=True) + NORM_EPS)
        qlat_ref[...] = (y * qan_ref[...]).astype(BF16)

    q = _dot(qlat_ref[...], wq_ref[...])
    q = q * lax.rsqrt(jnp.sum(q * q, axis=-1, keepdims=True) * (1.0 / MLA_QK) + NORM_EPS) * qn_ref[...]
    if rope:
        q = _rope(q, cos_ref[...], sp_ref[...], sn_ref[...], MLA_ROPE // 4)
    q_ref[...] = (q * MLA_SCALE).astype(BF16)


def _mla_q(uA, qan, wq, qn, tabs, *, ts, rope):
    B, S, _ = uA.shape
    tab_spec = pl.BlockSpec((ts, LANES), lambda b, s, h: (s, 0))
    return pl.pallas_call(
        functools.partial(_mla_q_kernel, rope=rope),
        grid=(B, S // ts, MLA_HEADS),
        in_specs=[pl.BlockSpec((None, ts, MLA_Q_RANK), lambda b, s, h: (b, s, 0)),
                  pl.BlockSpec((1, MLA_Q_RANK), lambda b, s, h: (0, 0)),
                  pl.BlockSpec((None, MLA_Q_RANK, LANES), lambda b, s, h: (h, 0, 0)),
                  pl.BlockSpec((1, LANES), lambda b, s, h: (0, 0)),
                  tab_spec, tab_spec, tab_spec],
        out_specs=pl.BlockSpec((None, None, ts, LANES), lambda b, s, h: (b, h, s, 0)),
        out_shape=jax.ShapeDtypeStruct((B, MLA_HEADS, S, LANES), BF16),
        scratch_shapes=[pltpu.VMEM((ts, MLA_Q_RANK), BF16)],
        compiler_params=_cparams(("parallel", "parallel", "arbitrary")),
        name="mla_q",
    )(uA, qan, wq, qn, *tabs)


def _mla_kv_kernel(kva_ref, kr_ref, kvn_ref, wk_ref, wv_ref, kn_ref, cos_ref, sp_ref, sn_ref,
                   k_ref, v_ref, ckv_ref, ckvb_ref, *, rope, normalise):
    h = pl.program_id(2)

    @pl.when(h == 0)
    def _():
        c = kva_ref[...]
        if normalise:
            c = c * lax.rsqrt(jnp.mean(c * c, axis=-1, keepdims=True) + NORM_EPS) * kvn_ref[...]
        ckv_ref[...] = c
        ckvb_ref[...] = c.astype(BF16)

    c = ckvb_ref[...]
    k = _dot(c, wk_ref[...]) + kr_ref[...]
    k = k * lax.rsqrt(jnp.sum(k * k, axis=-1, keepdims=True) * (1.0 / MLA_QK) + NORM_EPS) * kn_ref[...]
    if rope:
        k = _rope(k, cos_ref[...], sp_ref[...], sn_ref[...], MLA_ROPE // 4)
    k_ref[...] = k.astype(BF16)
    v_ref[...] = _dot(c, wv_ref[...]).astype(BF16)


def _mla_kv(kva_src, kva_blk, kr_src, kr_blk, kvn, wk, wv, kn, tabs, *, ts, rope, normalise):
    B, S, _ = kva_src.shape
    tab_spec = pl.BlockSpec((ts, LANES), lambda b, s, h: (s, 0))
    head_out = pl.BlockSpec((None, None, ts, LANES), lambda b, s, h: (b, h, s, 0))
    return pl.pallas_call(
        functools.partial(_mla_kv_kernel, rope=rope, normalise=normalise),
        grid=(B, S // ts, MLA_HEADS),
        in_specs=[pl.BlockSpec((None, ts, LANES), lambda b, s, h: (b, s, kva_blk)),
                  pl.BlockSpec((None, ts, LANES), lambda b, s, h: (b, s, kr_blk)),
                  pl.BlockSpec((1, LANES), lambda b, s, h: (0, 0)),
                  pl.BlockSpec((None, LANES, LANES), lambda b, s, h: (h, 0, 0)),
                  pl.BlockSpec((None, LANES, LANES), lambda b, s, h: (h, 0, 0)),
                  pl.BlockSpec((1, LANES), lambda b, s, h: (0, 0)),
                  tab_spec, tab_spec, tab_spec],
        out_specs=[head_out, head_out, pl.BlockSpec((None, ts, LANES), lambda b, s, h: (b, s, 0))],
        out_shape=[jax.ShapeDtypeStruct((B, MLA_HEADS, S, LANES), BF16),
                   jax.ShapeDtypeStruct((B, MLA_HEADS, S, LANES), BF16),
                   jax.ShapeDtypeStruct((B, S, LANES), F32)],
        scratch_shapes=[pltpu.VMEM((ts, LANES), BF16)],
        compiler_params=_cparams(("parallel", "parallel", "arbitrary")),
        name="mla_kv",
    )(kva_src, kr_src, kvn, wk, wv, kn, *tabs)


def _attn_kernel(q_ref, k_ref, v_ref, sink_ref, o_ref, *, use_sink):
    s = _dot_nt(q_ref[...], k_ref[...])
    m = jnp.max(s, axis=-1, keepdims=True)
    if use_sink:
        snk = sink_ref[...][:, 0:1]
        m = jnp.maximum(m, snk)
    p = jnp.exp(s - m)
    l = jnp.sum(p, axis=-1, keepdims=True)
    if use_sink:
        l = l + jnp.exp(snk - m)
    o = _dot(p.astype(BF16), v_ref[...])
    o_ref[...] = (o / l).astype(o_ref.dtype)


def _attn(q, k, v, sink, *, tq, use_sink):
    B, H, Sq, _ = q.shape
    G, Sk = k.shape[1], k.shape[2]
    rep = H // G
    return pl.pallas_call(
        functools.partial(_attn_kernel, use_sink=use_sink),
        grid=(B, H, Sq // tq),
        in_specs=[pl.BlockSpec((None, None, tq, LANES), lambda b, h, i: (b, h, i, 0)),
                  pl.BlockSpec((None, None, Sk, LANES), lambda b, h, i: (b, h // rep, 0, 0)),
                  pl.BlockSpec((None, None, Sk, LANES), lambda b, h, i: (b, h // rep, 0, 0)),
                  pl.BlockSpec((None, 1, LANES), lambda b, h, i: (h, 0, 0))],
        out_specs=pl.BlockSpec((None, tq, LANES), lambda b, h, i: (b, i, h)),
        out_shape=jax.ShapeDtypeStruct((B, Sq, H * LANES), BF16),
        compiler_params=_cparams(("parallel", "parallel", "parallel")),
        name="attn",
    )(q, k, v, sink)


def _win_attn_kernel(q_ref, kp_ref, kc_ref, kn_ref, vp_ref, vc_ref, vn_ref, kx_ref, vx_ref, sink_ref, o_ref):
    i = pl.program_id(2)
    nb = pl.num_programs(2)
    blk = SW_BLOCK
    kk = jnp.concatenate([kp_ref[...], kc_ref[...], kn_ref[...], kx_ref[...]], axis=0)
    vv = jnp.concatenate([vp_ref[...], vc_ref[...], vn_ref[...], vx_ref[...]], axis=0)
    nk = kk.shape[0]
    qi = lax.broadcasted_iota(jnp.int32, (blk, nk), 0)
    kj = lax.broadcasted_iota(jnp.int32, (blk, nk), 1)
    first = (i == 0).astype(jnp.int32)
    last = (i == nb - 1).astype(jnp.int32)
    lo = qi * (1 - first) + blk * first
    hi = (qi + 2 * blk) * (1 - last) + (2 * blk - 1) * last
    mask = (kj >= lo) & ((kj <= hi) | (kj >= 3 * blk))
    for r in range(SW_REP):
        s = _dot_nt(q_ref[r], kk)
        s = jnp.where(mask, s, NEG_BIG)
        snk = sink_ref[r][:, 0:1]
        m = jnp.maximum(jnp.max(s, axis=-1, keepdims=True), snk)
        p = jnp.exp(s - m)
        l = jnp.sum(p, axis=-1, keepdims=True) + jnp.exp(snk - m)
        o = _dot(p.astype(BF16), vv)
        o_ref[:, r * LANES:(r + 1) * LANES] = (o / l).astype(o_ref.dtype)


def _win_attn(q, k, v, kx, vx, sink):
    B, H, S, _ = q.shape
    G = k.shape[1]
    C = kx.shape[2]
    nb = S // SW_BLOCK
    blk = SW_BLOCK

    def kv_spec(off):
        return pl.BlockSpec((None, None, blk, LANES),
                            lambda b, g, i: (b, g, jnp.clip(i + off, 0, nb - 1), 0))

    ctx_spec = pl.BlockSpec((None, None, C, LANES), lambda b, g, i: (b, g, 0, 0))
    return pl.pallas_call(
        _win_attn_kernel,
        grid=(B, G, nb),
        in_specs=[pl.BlockSpec((None, SW_REP, blk, LANES), lambda b, g, i: (b, g, i, 0)),
                  kv_spec(-1), kv_spec(0), kv_spec(1), kv_spec(-1), kv_spec(0), kv_spec(1),
                  ctx_spec, ctx_spec,
                  pl.BlockSpec((None, SW_REP, 1, LANES), lambda b, g, i: (g, 0, 0, 0))],
        out_specs=pl.BlockSpec((None, blk, SW_REP * LANES), lambda b, g, i: (b, i, g)),
        out_shape=jax.ShapeDtypeStruct((B, S, H * LANES), BF16),
        compiler_params=_cparams(("parallel", "parallel", "parallel")),
        name="win_attn",
    )(q, k, k, k, v, v, v, kx, vx, sink)


def _swa_prep_kernel(u_ref, qn_ref, kn_ref, cos_ref, sp_ref, sn_ref, q_ref, k_ref, v_ref, kraw_ref, *, rope):
    def norm(x, g):
        return x * lax.rsqrt(jnp.sum(x * x, axis=-1, keepdims=True) * (1.0 / SW_HD) + NORM_EPS) * g

    def rot(x):
        if not rope:
            return x
        return _rope(x, cos_ref[...], sp_ref[...], sn_ref[...], SW_HD // 4)

    for h in range(SW_HEADS):
        x = norm(u_ref[:, h * LANES:(h + 1) * LANES], qn_ref[...])
        q_ref[h] = (rot(x) * SW_SCALE).astype(BF16)
    k0 = SW_HEADS * LANES
    v0 = k0 + SW_KV_HEADS * LANES
    for g in range(SW_KV_HEADS):
        x = norm(u_ref[:, k0 + g * LANES:k0 + (g + 1) * LANES], kn_ref[...])
        kraw_ref[:, g * LANES:(g + 1) * LANES] = x
        k_ref[g] = rot(x).astype(BF16)
        v_ref[g] = u_ref[:, v0 + g * LANES:v0 + (g + 1) * LANES].astype(BF16)


def _swa_prep(uC, qn, kn, tabs, *, ts, rope):
    B, S, _ = uC.shape
    tab_spec = pl.BlockSpec((ts, LANES), lambda b, s: (s, 0))
    return pl.pallas_call(
        functools.partial(_swa_prep_kernel, rope=rope),
        grid=(B, S // ts),
        in_specs=[pl.BlockSpec((None, ts, WC), lambda b, s: (b, s, 0)),
                  pl.BlockSpec((1, LANES), lambda b, s: (0, 0)),
                  pl.BlockSpec((1, LANES), lambda b, s: (0, 0)),
                  tab_spec, tab_spec, tab_spec],
        out_specs=[pl.BlockSpec((None, SW_HEADS, ts, LANES), lambda b, s: (b, 0, s, 0)),
                   pl.BlockSpec((None, SW_KV_HEADS, ts, LANES), lambda b, s: (b, 0, s, 0)),
                   pl.BlockSpec((None, SW_KV_HEADS, ts, LANES), lambda b, s: (b, 0, s, 0)),
                   pl.BlockSpec((None, ts, SW_KV_HEADS * LANES), lambda b, s: (b, s, 0))],
        out_shape=[jax.ShapeDtypeStruct((B, SW_HEADS, S, LANES), BF16),
                   jax.ShapeDtypeStruct((B, SW_KV_HEADS, S, LANES), BF16),
                   jax.ShapeDtypeStruct((B, SW_KV_HEADS, S, LANES), BF16),
                   jax.ShapeDtypeStruct((B, S, SW_KV_HEADS * LANES), F32)],
        compiler_params=_cparams(("parallel", "parallel")),
        name="swa_prep",
    )(uC, qn, kn, *tabs)


def _dir_masks(L, d):
    row = lax.broadcasted_iota(jnp.int32, (L, L), 0)
    col = lax.broadcasted_iota(jnp.int32, (L, L), 1)
    diff = (col - row) * (1 - 2 * d)
    return diff <= 0, diff < 0


def _mlstm_kernel(qkv_ref, gc_ref, gr_ref, bc_ref, br_ref, C0_ref, n0_ref, m0_ref,
                  h_ref, Cout_ref, nout_ref, mout_ref, C_s, n_s, m_s):
    d = pl.program_id(1)
    c = pl.program_id(2)
    L = ML_CHUNK
    H = ML_HEADS

    @pl.when(c == 0)
    def _():
        C_s[...] = C0_ref[...]
        n_s[...] = n0_ref[...]
        m_s[...] = m0_ref[...]

    incl, _ = _dir_masks(L, d)
    incl_b = jnp.where(incl, 1.0, 0.0).astype(BF16)
    incl_t, _ = _dir_masks(L, 1 - d)
    incl_t_b = jnp.where(incl_t, 1.0, 0.0).astype(BF16)

    gcol = gc_ref[...] + bc_ref[...]
    grow = gr_ref[...] + br_ref[...]
    lane = lax.broadcasted_iota(jnp.int32, gcol.shape, 1)
    gcol = jnp.where(lane < H, gcol, _log_sigmoid(gcol))
    rowi = lax.broadcasted_iota(jnp.int32, grow.shape, 0)
    grow = jnp.where(rowi < H, grow, _log_sigmoid(grow))
    bcol = _mask_dot(incl_b, gcol)
    brow = _dot_mask(grow, incl_t_b)
    tot = jnp.sum(gcol, axis=0, keepdims=True)

    for hh in range(H):
        q = qkv_ref[:, hh * LANES:(hh + 1) * LANES] * (ML_DK ** -0.5)
        k = qkv_ref[:, (H + hh) * LANES:(H + hh + 1) * LANES]
        v = qkv_ref[:, (2 * H + hh) * LANES:(2 * H + hh + 1) * LANES]
        qb, kb, vb = q.astype(BF16), k.astype(BF16), v.astype(BF16)
        i_col = gcol[:, hh:hh + 1]
        b_col = bcol[:, H + hh:H + hh + 1]
        i_row = grow[hh:hh + 1, :]
        b_row = brow[H + hh:H + hh + 1, :]
        bL = tot[:, H + hh:H + hh + 1]
        C = C_s[hh]
        n = n_s[hh]
        m = m_s[hh][:, 0:1]

        D = jnp.where(incl, b_col - b_row + i_row, -jnp.inf)
        inter = b_col + m
        m_t = jnp.maximum(inter, jnp.max(D, axis=-1, keepdims=True))
        W = _dot_nt(qb, kb) * jnp.exp(D - m_t)
        a_in = jnp.exp(inter - m_t)
        num = _dot(W.astype(BF16), vb) + a_in * _dot(qb, C.astype(BF16))
        den = jnp.sum(W, axis=-1, keepdims=True) + a_in * jnp.sum(q * n, axis=-1, keepdims=True)
        h_ref[:, hh * LANES:(hh + 1) * LANES] = num / jnp.maximum(jnp.abs(den), jnp.exp(-m_t))

        g_row = bL - b_row + i_row
        m_new = jnp.maximum(bL + m, jnp.max(g_row, axis=-1, keepdims=True))
        wk = jnp.exp(bL - b_col + i_col - m_new)
        decay = jnp.exp(bL + m - m_new)
        ks = k * wk
        C_s[hh] = decay * C + _dot_tn(ks.astype(BF16), vb)
        n_s[hh] = decay * n + jnp.sum(ks, axis=0, keepdims=True)
        m_s[hh] = jnp.broadcast_to(m_new, (1, LANES))

    @pl.when(c == pl.num_programs(2) - 1)
    def _():
        Cout_ref[...] = C_s[...]
        nout_ref[...] = n_s[...]
        mout_ref[...] = m_s[...]


def _mlstm(uB, g_rows, bias_col, bias_row, C0, n0, m0):
    B, S, _ = uB.shape
    L, H = ML_CHUNK, ML_HEADS
    nc = S // L
    gate_blk = 4 * 512 // LANES

    def ceff(d, c):
        return c + d * (nc - 1 - 2 * c)

    st_spec = lambda shape: pl.BlockSpec((None, None) + shape, lambda b, d, c: (b, d) + (0,) * len(shape))
    return pl.pallas_call(
        _mlstm_kernel,
        grid=(B, 2, nc),
        in_specs=[pl.BlockSpec((None, L, 3 * 512), lambda b, d, c: (b, ceff(d, c), 0)),
                  pl.BlockSpec((None, L, LANES), lambda b, d, c: (b, ceff(d, c), gate_blk + d)),
                  pl.BlockSpec((None, None, None, 2 * H, L), lambda b, d, c: (b, d, ceff(d, c), 0, 0)),
                  pl.BlockSpec((None, 1, LANES), lambda b, d, c: (d, 0, 0)),
                  pl.BlockSpec((None, 2 * H, L), lambda b, d, c: (d, 0, 0)),
                  st_spec((H, LANES, LANES)), st_spec((H, 1, LANES)), st_spec((H, 1, LANES))],
        out_specs=[pl.BlockSpec((None, None, L, H * LANES), lambda b, d, c: (d, b, ceff(d, c), 0)),
                   st_spec((H, LANES, LANES)), st_spec((H, 1, LANES)), st_spec((H, 1, LANES))],
        out_shape=[jax.ShapeDtypeStruct((2, B, S, H * LANES), F32),
                   jax.ShapeDtypeStruct((B, 2, H, LANES, LANES), F32),
                   jax.ShapeDtypeStruct((B, 2, H, 1, LANES), F32),
                   jax.ShapeDtypeStruct((B, 2, H, 1, LANES), F32)],
        scratch_shapes=[pltpu.VMEM((H, LANES, LANES), F32),
                        pltpu.VMEM((H, 1, LANES), F32),
                        pltpu.VMEM((H, 1, LANES), F32)],
        compiler_params=_cparams(("parallel", "parallel", "arbitrary")),
        name="mlstm",
    )(uB, uB, g_rows, bias_col, bias_row, C0, n0, m0)


def _mlstm_post_kernel(hf_ref, hb_ref, o_ref, g_ref, y_ref):
    for hh in range(ML_HEADS):
        sl = slice(hh * LANES, (hh + 1) * LANES)
        x = hf_ref[:, sl] + hb_ref[:, sl]
        y = x * lax.rsqrt(jnp.mean(x * x, axis=-1, keepdims=True) + NORM_EPS) * g_ref[...]
        y_ref[:, sl] = (y * _sigmoid(o_ref[:, sl])).astype(y_ref.dtype)


def _mlstm_post(hdir, uB, g, *, ts):
    _, B, S, W = hdir.shape
    return pl.pallas_call(
        _mlstm_post_kernel,
        grid=(B, S // ts),
        in_specs=[pl.BlockSpec((None, None, ts, W), lambda b, s: (0, b, s, 0)),
                  pl.BlockSpec((None, None, ts, W), lambda b, s: (1, b, s, 0)),
                  pl.BlockSpec((None, ts, W), lambda b, s: (b, s, 3)),
                  pl.BlockSpec((1, LANES), lambda b, s: (0, 0))],
        out_specs=pl.BlockSpec((None, ts, W), lambda b, s: (b, s, 0)),
        out_shape=jax.ShapeDtypeStruct((B, S, W), BF16),
        compiler_params=_cparams(("parallel", "parallel")),
        name="mlstm_post",
    )(hdir, hdir, uB, g)


def _rwkv_kernel(rkv_ref, w1_ref, a1_ref, w2_ref, w0_ref, a2_ref, a0_ref, kk_ref, ka_ref, S0_ref,
                 y_ref, Sout_ref, S_s):
    d = pl.program_id(1)
    c = pl.program_id(2)
    T = RW_CHUNK
    HW = RW_HEADS * LANES

    @pl.when(c == 0)
    def _():
        S_s[...] = S0_ref[...]

    incl, strict = _dir_masks(T, d)
    incl_b = jnp.where(incl, 1.0, 0.0).astype(BF16)
    eye = (lax.broadcasted_iota(jnp.int32, (T, T), 0) == lax.broadcasted_iota(jnp.int32, (T, T), 1)).astype(F32)

    r = rkv_ref[:, 0:HW]
    k = rkv_ref[:, HW:2 * HW]
    v = rkv_ref[:, 2 * HW:3 * HW]
    w_pre = _dot(jnp.tanh(w1_ref[...]).astype(BF16), w2_ref[...]) + w0_ref[...]
    logw = -RW_DECAY_SCALE * _sigmoid(w_pre)
    a = _sigmoid(_dot(a1_ref[...].astype(BF16), a2_ref[...]) + a0_ref[...])
    kappa = k * kk_ref[...]
    kt = k * (1.0 + (a - 1.0) * ka_ref[...])
    cs = _mask_dot(incl_b, logw)
    tot = jnp.sum(logw, axis=0, keepdims=True)
    e_pos = jnp.exp(cs)
    e_neg = jnp.exp(-cs)
    e_prev = jnp.exp(cs - logw)
    e_tail = jnp.exp(tot - cs)
    g_tot = jnp.exp(tot)

    for hh in range(RW_HEADS):
        sl = slice(hh * LANES, (hh + 1) * LANES)
        kap = kappa[:, sl]
        kh = kap * lax.rsqrt(jnp.sum(kap * kap, axis=-1, keepdims=True) + 1e-12)
        bb = a[:, sl] * kh
        kth = kt[:, sl]
        Q = (kh * e_prev[:, sl]).astype(BF16)
        Kh = (kth * e_neg[:, sl]).astype(BF16)
        Bh = (bb * e_neg[:, sl]).astype(BF16)
        R = (r[:, sl] * e_pos[:, sl]).astype(BF16)
        Kb = (kth * e_tail[:, sl]).astype(BF16)
        Bb = (bb * e_tail[:, sl]).astype(BF16)
        V = v[:, sl].astype(BF16)

        Lk = jnp.where(strict, _dot_nt(Q, Kh), 0.0)
        Ak = jnp.where(incl, _dot_nt(R, Kh), 0.0)
        Ab = jnp.where(incl, _dot_nt(R, Bh), 0.0)
        N = jnp.where(strict, -_dot_nt(Q, Bh), 0.0)
        X = eye + N
        P = N
        for _ in range(5):
            Pb = P.astype(BF16)
            P = _dot(Pb, Pb)
            X = X + _dot(X.astype(BF16), P.astype(BF16))

        S0 = S_s[hh]
        S0b = S0.astype(BF16)
        rhs = _dot_nt(Q, S0b) + _dot(Lk.astype(BF16), V)
        Z = _dot(X.astype(BF16), rhs.astype(BF16))
        Zb = Z.astype(BF16)
        y_ref[:, sl] = _dot_nt(R, S0b) + _dot(Ak.astype(BF16), V) - _dot(Ab.astype(BF16), Zb)
        S_s[hh] = S0 * g_tot[:, sl] + _dot_tn(V, Kb) - _dot_tn(Zb, Bb)

    @pl.when(c == pl.num_programs(2) - 1)
    def _():
        Sout_ref[...] = S_s[...]


def _rwkv(uD, w2, w0, a2, a0, kk, ka, S0):
    B, S, _ = uD.shape
    T = RW_CHUNK
    nc = S // T
    HW = RW_HEADS * LANES

    def ceff(d, c):
        return c + d * (nc - 1 - 2 * c)

    st_spec = pl.BlockSpec((None, None, RW_HEADS, LANES, LANES), lambda b, d, c: (b, d, 0, 0, 0))
    return pl.pallas_call(
        _rwkv_kernel,
        grid=(B, 2, nc),
        in_specs=[pl.BlockSpec((None, T, 3 * HW), lambda b, d, c: (b, ceff(d, c), 0)),
                  pl.BlockSpec((None, T, LANES), lambda b, d, c: (b, ceff(d, c), RW_W1_BLK + d)),
                  pl.BlockSpec((None, T, LANES), lambda b, d, c: (b, ceff(d, c), RW_A1_BLK + d)),
                  pl.BlockSpec((None, LANES, HW), lambda b, d, c: (d, 0, 0)),
                  pl.BlockSpec((None, 1, HW), lambda b, d, c: (d, 0, 0)),
                  pl.BlockSpec((None, LANES, HW), lambda b, d, c: (d, 0, 0)),
                  pl.BlockSpec((None, 1, HW), lambda b, d, c: (d, 0, 0)),
                  pl.BlockSpec((1, HW), lambda b, d, c: (0, 0)),
                  pl.BlockSpec((1, HW), lambda b, d, c: (0, 0)),
                  st_spec],
        out_specs=[pl.BlockSpec((None, None, T, HW), lambda b, d, c: (d, b, ceff(d, c), 0)),
                   st_spec],
        out_shape=[jax.ShapeDtypeStruct((2, B, S, HW), F32),
                   jax.ShapeDtypeStruct((B, 2, RW_HEADS, LANES, LANES), F32)],
        scratch_shapes=[pltpu.VMEM((RW_HEADS, LANES, LANES), F32)],
        compiler_params=_cparams(("parallel", "parallel", "arbitrary")),
        name="rwkv",
    )(uD, uD, uD, w2, w0, a2, a0, kk, ka, S0)


def _rwkv_post_kernel(yf_ref, yb_ref, u_ref, a2_ref, a0_ref, ka_ref, uu_ref, g2_ref, gng_ref, gnb_ref, o_ref):
    HW = RW_HEADS * LANES
    r = u_ref[:, 0:HW]
    k = u_ref[:, HW:2 * HW]
    v = u_ref[:, 2 * HW:3 * HW]
    a1_0 = RW_A1_BLK * LANES
    g1_0 = a1_0 + 2 * LANES
    g = _dot(_sigmoid(u_ref[:, g1_0:g1_0 + LANES]).astype(BF16), g2_ref[...])
    rk = []
    for dd in range(2):
        a1 = u_ref[:, a1_0 + dd * LANES:a1_0 + (dd + 1) * LANES]
        a = _sigmoid(_dot(a1.astype(BF16), a2_ref[dd]) + a0_ref[dd])
        kt = k * (1.0 + (a - 1.0) * ka_ref[...])
        rk.append(r * kt * uu_ref[dd])
    y = yf_ref[...] + yb_ref[...]
    lane = lax.broadcasted_iota(jnp.int32, (1, LANES), 1)
    real = lane < RW_N
    for hh in range(RW_HEADS):
        sl = slice(hh * LANES, (hh + 1) * LANES)
        yh = y[:, sl]
        mu = jnp.sum(yh, axis=-1, keepdims=True) * (1.0 / RW_N)
        dev = jnp.where(real, yh - mu, 0.0)
        var = jnp.sum(dev * dev, axis=-1, keepdims=True) * (1.0 / RW_N)
        yn = dev * lax.rsqrt(var + RW_GN_EPS) * gng_ref[:, sl] + gnb_ref[:, sl]
        bonus = (jnp.sum(rk[0][:, sl], axis=-1, keepdims=True)
                 + jnp.sum(rk[1][:, sl], axis=-1, keepdims=True)) * v[:, sl]
        o_ref[:, sl] = ((yn + bonus) * g[:, sl]).astype(o_ref.dtype)


def _rwkv_post(ydir, uD, a2, a0, ka, uu, g2, gng, gnb, *, ts):
    _, B, S, HW = ydir.shape
    full = lambda shape: pl.BlockSpec(shape, lambda b, s: (0,) * len(shape))
    return pl.pallas_call(
        _rwkv_post_kernel,
        grid=(B, S // ts),
        in_specs=[pl.BlockSpec((None, None, ts, HW), lambda b, s: (0, b, s, 0)),
                  pl.BlockSpec((None, None, ts, HW), lambda b, s: (1, b, s, 0)),
                  pl.BlockSpec((None, ts, WD), lambda b, s: (b, s, 0)),
                  full((2, LANES, HW)), full((2, 1, HW)), full((1, HW)), full((2, 1, HW)),
                  full((LANES, HW)), full((1, HW)), full((1, HW))],
        out_specs=pl.BlockSpec((None, ts, HW), lambda b, s: (b, s, 0)),
        out_shape=jax.ShapeDtypeStruct((B, S, HW), BF16),
        compiler_params=_cparams(("parallel", "parallel")),
        name="rwkv_post",
    )(ydir, ydir, uD, a2, a0, ka, uu, g2, gng, gnb)


def _pad_heads(w, nh, hd, axis=-1):
    axis = axis % w.ndim
    shp = w.shape
    w = w.reshape(shp[:axis] + (nh, hd) + shp[axis + 1:])
    pad = [(0, 0)] * w.ndim
    pad[axis + 1] = (0, LANES - hd)
    w = jnp.pad(w, pad)
    return w.reshape(shp[:axis] + (nh * LANES,) + shp[axis + 1:])


def _pad_cols(w, width):
    return jnp.pad(w, [(0, 0)] * (w.ndim - 1) + [(0, width - w.shape[-1])])


def _layer_weights(P, l):
    w_in = P["w_in"][l]
    col = lambda i: w_in[:, _IN_OFFS[i]:_IN_OFFS[i + 1]]
    zc = lambda n: jnp.zeros((D_MODEL, n), F32)
    W = {}
    W["wA"] = jnp.concatenate([col(0), col(1), zc(MLA_NOPE), col(2), zc(LANES - MLA_NOPE - MLA_ROPE)], 1).astype(BF16)
    gi, gf = col(6), col(7)
    gates = [jnp.concatenate([gi[:, d * ML_HEADS:(d + 1) * ML_HEADS], gf[:, d * ML_HEADS:(d + 1) * ML_HEADS],
                              zc(LANES - 2 * ML_HEADS)], 1) for d in range(2)]
    W["wB"] = jnp.concatenate([_pad_heads(col(3), ML_HEADS, ML_DK), _pad_heads(col(4), ML_HEADS, ML_DK),
                               col(5), col(8)] + gates, 1).astype(BF16)
    W["wC"] = jnp.concatenate([_pad_heads(col(9), SW_HEADS, SW_HD), _pad_heads(col(10), SW_KV_HEADS, SW_HD),
                               _pad_heads(col(11), SW_KV_HEADS, SW_HD)], 1).astype(BF16)
    W["wD"] = jnp.concatenate([_pad_heads(col(12), RW_HEADS, RW_N), _pad_heads(col(13), RW_HEADS, RW_N),
                               _pad_heads(col(14), RW_HEADS, RW_N),
                               _pad_heads(col(15), 2, RW_W_RANK), _pad_heads(col(16), 2, RW_A_RANK),
                               col(17), zc(LANES)], 1).astype(BF16)
    W["wG"] = jnp.transpose(col(18).reshape(D_MODEL, N_BRANCH, D_MODEL), (1, 0, 2)).astype(BF16)
    wo_a = _pad_heads(P["mla_w_o"][l], MLA_HEADS, MLA_V, axis=0)
    wo_b = jnp.pad(P["mlstm_w_o"][l], ((0, D_MODEL - ML_HEADS * ML_DV), (0, 0)))
    wo_c = _pad_heads(P["swa_w_o"][l], SW_HEADS, SW_HD, axis=0)
    wo_d = _pad_heads(P["rwkv_w_o"][l], RW_HEADS, RW_N, axis=0)
    W["wO"] = jnp.stack([wo_a, wo_b, wo_c, wo_d]).astype(BF16)
    W["w_out"] = P["w_out"][l].astype(BF16)
    W["mlp_w1"] = P["mlp_w1"][l].astype(BF16)
    W["mlp_w2"] = P["mlp_w2"][l].astype(BF16)
    W["norm1"] = P["norm1"][l][None, :]
    W["norm2"] = P["norm2"][l][None, :]
    W["q_a_norm"] = P["mla_q_a_norm"][l][None, :]
    W["kv_a_norm"] = P["mla_kv_a_norm"][l][None, :]
    wq = P["mla_w_uq"][l].reshape(MLA_Q_RANK, MLA_HEADS, MLA_QK)
    W["wq"] = jnp.transpose(_pad_cols(wq, LANES), (1, 0, 2)).astype(BF16)
    wkv = P["mla_w_ukv"][l].reshape(MLA_KV_RANK, MLA_HEADS, MLA_NOPE + MLA_V)
    W["wk"] = jnp.transpose(_pad_cols(wkv[..., :MLA_NOPE], LANES), (1, 0, 2)).astype(BF16)
    W["wv"] = jnp.transpose(_pad_cols(wkv[..., MLA_NOPE:], LANES), (1, 0, 2)).astype(BF16)
    W["q_norm"] = _pad_cols(P["mla_q_norm"][l][None, :], LANES)
    W["k_norm"] = _pad_cols(P["mla_k_norm"][l][None, :], LANES)
    ib, fb = P["mlstm_i_bias"][l], P["mlstm_f_bias"][l]
    bias = jnp.concatenate([ib, fb], -1)
    W["ml_bias_col"] = _pad_cols(bias, LANES)[:, None, :]
    W["ml_bias_row"] = jnp.broadcast_to(bias[:, :, None], (2, 2 * ML_HEADS, ML_CHUNK))
    W["ml_norm"] = P["mlstm_norm"][l][None, :]
    W["swa_q_norm"] = _pad_cols(P["swa_q_norm"][l][None, :], LANES)
    W["swa_k_norm"] = _pad_cols(P["swa_k_norm"][l][None, :], LANES)
    sink = jnp.broadcast_to(P["swa_sink"][l][:, None, None], (SW_HEADS, 1, LANES))
    W["sink"] = sink
    W["sink_g"] = sink.reshape(SW_KV_HEADS, SW_REP, 1, LANES)
    ph = lambda w: _pad_heads(w, RW_HEADS, RW_N)
    W["rw_w2"] = jnp.pad(ph(P["rwkv_w2"][l]), ((0, 0), (0, LANES - RW_W_RANK), (0, 0))).astype(BF16)
    W["rw_a2"] = jnp.pad(ph(P["rwkv_a2"][l]), ((0, 0), (0, LANES - RW_A_RANK), (0, 0))).astype(BF16)
    W["rw_w0"] = ph(P["rwkv_w0"][l])[:, None, :]
    W["rw_a0"] = ph(P["rwkv_a0"][l])[:, None, :]
    W["rw_g2"] = ph(P["rwkv_g2"][l]).astype(BF16)
    W["rw_kk"] = ph(P["rwkv_kk"][l])[None, :]
    W["rw_ka"] = ph(P["rwkv_ka"][l])[None, :]
    W["rw_u"] = ph(P["rwkv_u"][l])[:, None, :]
    W["rw_gn_g"] = ph(P["rwkv_gn_g"][l])[None, :]
    W["rw_gn_b"] = ph(P["rwkv_gn_b"][l])[None, :]
    return W


def _tile(n, pref):
    t = min(n, pref)
    assert n % t == 0, (n, t)
    return t


def _trunk_layer(x, mod, W, ctx, B, S):
    Bm, Sm, D = x.shape
    ntok = Bm * Sm
    ts = _tile(Sm, 1024)
    tm = _tile(ntok, 1024)
    tsq = _tile(S, 256)
    latent = ctx is not None

    h = _norm_mod(x, W["norm1"], mod, 0, 1, ts=ts)
    hf = h.reshape(ntok, D)
    uA = _mm(hf, W["wA"], tm=tm, tn=WA, name="proj_a").reshape(B, S, WA)
    uB = _mm(hf, W["wB"], tm=tm, tn=768, name="proj_b").reshape(B, S, WB)
    uC = _mm(hf, W["wC"], tm=tm, tn=768, name="proj_c").reshape(B, S, WC)
    uD = _mm(hf, W["wD"], tm=tm, tn=768, name="proj_d").reshape(B, S, WD)

    if latent:
        tabs = _rope_tables(S, MLA_NOPE, MLA_ROPE)
    else:
        tabs = tuple(jnp.zeros((S, LANES), F32) for _ in range(3))
    q_mla = _mla_q(uA, W["q_a_norm"], W["wq"], W["q_norm"], tabs, ts=tsq, rope=latent)
    k_mla, v_mla, c_kv = _mla_kv(uA, MLA_Q_RANK // LANES, uA, MLA_Q_RANK // LANES + 1, W["kv_a_norm"],
                                 W["wk"], W["wv"], W["k_norm"], tabs, ts=tsq, rope=latent, normalise=True)
    if latent:
        C = ctx["mla_ckv"].shape[1]
        ctabs = tuple(jnp.zeros((C, LANES), F32) for _ in range(3))
        kr_c = jnp.pad(ctx["mla_krope"], ((0, 0), (0, 0), (MLA_NOPE, LANES - MLA_NOPE - MLA_ROPE)))
        k_c, v_c, _ = _mla_kv(ctx["mla_ckv"], 0, kr_c, 0, W["kv_a_norm"], W["wk"], W["wv"], W["k_norm"],
                              ctabs, ts=C, rope=False, normalise=False)
        k_mla = jnp.concatenate([k_mla, k_c], 2)
        v_mla = jnp.concatenate([v_mla, v_c], 2)
    y_a = _attn(q_mla, k_mla, v_mla, W["sink"], tq=tsq, use_sink=False)

    L = ML_CHUNK
    nc = S // L
    g0 = 4 * 512
    g_rows = uB[:, :, g0:g0 + 2 * LANES].reshape(B, nc, L, 2, LANES)[..., :2 * ML_HEADS]
    g_rows = jnp.transpose(g_rows, (0, 3, 1, 4, 2))
    if latent:
        C0 = jnp.pad(ctx["mlstm_C"], ((0, 0), (0, 0), (0, 0), (0, LANES - ML_DK), (0, 0)))
        n0 = _pad_cols(ctx["mlstm_n"], LANES)[:, :, :, None, :]
        m0 = jnp.broadcast_to(ctx["mlstm_m"][..., None, None], (B, 2, ML_HEADS, 1, LANES))
    else:
        C0 = jnp.zeros((B, 2, ML_HEADS, LANES, LANES), F32)
        n0 = jnp.zeros((B, 2, ML_HEADS, 1, LANES), F32)
        m0 = jnp.zeros((B, 2, ML_HEADS, 1, LANES), F32)
    hdir, C_f, n_f, m_f = _mlstm(uB, g_rows, W["ml_bias_col"], W["ml_bias_row"], C0, n0, m0)
    y_b = _mlstm_post(hdir, uB, W["ml_norm"], ts=_tile(S, 512))

    if latent:
        stabs = _rope_tables(S, 0, SW_HD)
    else:
        stabs = tuple(jnp.zeros((S, LANES), F32) for _ in range(3))
    sq, sk, sv, sk_raw = _swa_prep(uC, W["swa_q_norm"], W["swa_k_norm"], stabs, ts=tsq, rope=latent)
    if latent:
        kx = _pad_cols(jnp.transpose(ctx["swa_k"], (0, 2, 1, 3)), LANES).astype(BF16)
        vx = _pad_cols(jnp.transpose(ctx["swa_v"], (0, 2, 1, 3)), LANES).astype(BF16)
        y_c = _win_attn(sq, sk, sv, kx, vx, W["sink_g"])
    else:
        y_c = _attn(sq, sk, sv, W["sink"], tq=tsq, use_sink=True)

    if latent:
        S0 = jnp.pad(ctx["rwkv"], ((0, 0), (0, 0), (0, 0), (0, LANES - RW_N), (0, LANES - RW_N)))
    else:
        S0 = jnp.zeros((B, 2, RW_HEADS, LANES, LANES), F32)
    ydir, S_rw = _rwkv(uD, W["rw_w2"], W["rw_w0"], W["rw_a2"], W["rw_a0"], W["rw_kk"], W["rw_ka"], S0)
    y_d = _rwkv_post(ydir, uD, W["rw_a2"], W["rw_a0"], W["rw_ka"], W["rw_u"], W["rw_g2"],
                     W["rw_gn_g"], W["rw_gn_b"], ts=_tile(S, 256))

    tok = lambda y: y.reshape(Bm, Sm, y.shape[-1])
    x = _merge(h, tok(y_a), tok(y_b), tok(y_c), tok(y_d), x, mod, W["wG"], W["wO"], W["w_out"], ts=_tile(Sm, 512))
    h2 = _norm_mod(x, W["norm2"], mod, 3, 4, ts=ts)
    x = _mlp(h2, x, mod, W["mlp_w1"], W["mlp_w2"], ts=ts, tf=512)

    new_ctx = None
    if not latent:
        kr0 = MLA_Q_RANK + MLA_KV_RANK + MLA_NOPE
        v0 = (SW_HEADS + SW_KV_HEADS) * LANES
        new_ctx = {
            "mla_ckv": c_kv,
            "mla_krope": uA[:, :, kr0:kr0 + MLA_ROPE],
            "swa_k": sk_raw.reshape(B, S, SW_KV_HEADS, LANES)[..., :SW_HD],
            "swa_v": uC[:, :, v0:].reshape(B, S, SW_KV_HEADS, LANES)[..., :SW_HD],
            "mlstm_C": C_f[:, :, :, :ML_DK, :],
            "mlstm_n": n_f[:, :, :, 0, :ML_DK],
            "mlstm_m": m_f[:, :, :, 0, 0],
            "rwkv": S_rw[:, :, :, :RW_N, :RW_N],
        }
    return x, new_ctx


def kernel(x_prompt, x_sample, c, cache_mla_ckv, cache_mla_krope, cache_swa_k, cache_swa_v, state_mlstm_C, state_mlstm_n, state_mlstm_m, state_rwkv, c_ctx, ada_w, ada_b, norm1, norm2, w_in, mla_q_a_norm, mla_kv_a_norm, mla_w_uq, mla_w_ukv, mla_q_norm, mla_k_norm, mla_w_o, mlstm_i_bias, mlstm_f_bias, mlstm_norm, mlstm_w_o, swa_q_norm, swa_k_norm, swa_sink, swa_w_o, rwkv_w0, rwkv_w2, rwkv_a0, rwkv_a2, rwkv_g2, rwkv_kk, rwkv_ka, rwkv_u, rwkv_gn_g, rwkv_gn_b, rwkv_w_o, w_out, mlp_w1, mlp_w2):
    P = {'norm1': norm1, 'norm2': norm2, 'w_in': w_in,
         'mla_q_a_norm': mla_q_a_norm, 'mla_kv_a_norm': mla_kv_a_norm, 'mla_w_uq': mla_w_uq,
         'mla_w_ukv': mla_w_ukv, 'mla_q_norm': mla_q_norm, 'mla_k_norm': mla_k_norm, 'mla_w_o': mla_w_o,
         'mlstm_i_bias': mlstm_i_bias, 'mlstm_f_bias': mlstm_f_bias, 'mlstm_norm': mlstm_norm, 'mlstm_w_o': mlstm_w_o,
         'swa_q_norm': swa_q_norm, 'swa_k_norm': swa_k_norm, 'swa_sink': swa_sink, 'swa_w_o': swa_w_o,
         'rwkv_w0': rwkv_w0, 'rwkv_w2': rwkv_w2, 'rwkv_a0': rwkv_a0, 'rwkv_a2': rwkv_a2, 'rwkv_g2': rwkv_g2,
         'rwkv_kk': rwkv_kk, 'rwkv_ka': rwkv_ka, 'rwkv_u': rwkv_u, 'rwkv_gn_g': rwkv_gn_g, 'rwkv_gn_b': rwkv_gn_b,
         'rwkv_w_o': rwkv_w_o, 'w_out': w_out, 'mlp_w1': mlp_w1, 'mlp_w2': mlp_w2}
    depth = w_in.shape[0]
    Bc, Sc, D = x_prompt.shape
    Bl, Sl, _ = x_sample.shape

    cond = jnp.concatenate([c, c_ctx[None, :]], 0)
    rows = cond.shape[0]
    rows_pad = -(-rows // 16) * 16
    cond = jnp.pad(cond, ((0, rows_pad - rows), (0, 0)))

    xc = x_prompt.reshape(1, Bc * Sc, D)
    xl = x_sample
    states = []
    for l in range(depth):
        W = _layer_weights(P, l)
        mod = _mm(cond, ada_w[l].astype(BF16), ada_b[l][None, :], tm=rows_pad, tn=1024, pre="silu", name="ada")
        mod_l = mod[:Bl, None, :]
        mod_c = mod[Bl:Bl + 1, None, :]
        xc, st = _trunk_layer(xc, mod_c, W, None, Bc, Sc)
        states.append(st)
        ctx = {'mla_ckv': cache_mla_ckv[:, l], 'mla_krope': cache_mla_krope[:, l],
               'swa_k': cache_swa_k[:, l], 'swa_v': cache_swa_v[:, l],
               'mlstm_C': state_mlstm_C[:, l], 'mlstm_n': state_mlstm_n[:, l], 'mlstm_m': state_mlstm_m[:, l],
               'rwkv': state_rwkv[:, l]}
        xl, _ = _trunk_layer(xl, mod_l, W, ctx, Bl, Sl)

    y_prompt = xc.reshape(Bc, Sc, D)
    stack = lambda name: jnp.stack([s[name] for s in states], 1)
    return (y_prompt, xl, stack('mla_ckv'), stack('mla_krope'), stack('swa_k'), stack('swa_v'),
            stack('mlstm_C'), stack('mlstm_n'), stack('mlstm_m'), stack('rwkv'))
```

```python
import functools

import jax
import jax.numpy as jnp
import numpy as np
from jax import lax
from jax.experimental import pallas as pl
from jax.experimental.pallas import tpu as pltpu

F32 = jnp.float32
BF16 = jnp.bfloat16

LANES = 128
VMEM_LIMIT_BYTES = 56 * 1024 * 1024

D_MODEL = 1024
NORM_EPS = 1e-6
ROPE_BASE = 10000.0
GRID_W = 64
NEG_BIG = -1e30
N_MOD = 6
N_BRANCH = 4
D_FF = 4 * D_MODEL

MLA_HEADS, MLA_NOPE, MLA_ROPE, MLA_V = 8, 64, 32, 64
MLA_QK = MLA_NOPE + MLA_ROPE
MLA_Q_RANK, MLA_KV_RANK = 256, 128
MLA_SCALE = MLA_QK ** -0.5

ML_HEADS, ML_DK, ML_DV = 4, 64, 128
ML_CHUNK = 128

SW_HEADS, SW_KV_HEADS, SW_HD = 8, 2, 64
SW_REP = SW_HEADS // SW_KV_HEADS
SW_BLOCK = 128
SW_SCALE = SW_HD ** -0.5

RW_HEADS, RW_N = 8, 64
RW_DIM = RW_HEADS * RW_N
RW_W_RANK, RW_A_RANK, RW_G_RANK = 64, 64, 128
RW_DECAY_SCALE = 0.6065306597126334
RW_GN_EPS = 64e-5
RW_CHUNK = 64

_IN_WIDTHS = (
    MLA_Q_RANK, MLA_KV_RANK, MLA_ROPE,
    ML_HEADS * ML_DK, ML_HEADS * ML_DK, ML_HEADS * ML_DV,
    2 * ML_HEADS, 2 * ML_HEADS, ML_HEADS * ML_DV,
    SW_HEADS * SW_HD, SW_KV_HEADS * SW_HD, SW_KV_HEADS * SW_HD,
    RW_DIM, RW_DIM, RW_DIM, 2 * RW_W_RANK, 2 * RW_A_RANK, RW_G_RANK,
    N_BRANCH * D_MODEL,
)
_IN_OFFS = tuple(int(v) for v in np.concatenate([[0], np.cumsum(_IN_WIDTHS)]))

WA = 512
WB = 4 * 512 + 2 * LANES
WC = 1024 + 256 + 256
WD = 3 * 1024 + 256 + 256 + 128 + 128
RW_W1_BLK = 3 * 1024 // LANES
RW_A1_BLK = RW_W1_BLK + 2

LOG2E = 1.4426950408889634
V_HD = 64


def _cparams(sem):
    return pltpu.CompilerParams(dimension_semantics=sem, vmem_limit_bytes=VMEM_LIMIT_BYTES)


def _dot(a, b):
    return jnp.dot(a, b, preferred_element_type=F32)


def _dot_nt(a, b):
    return lax.dot_general(a, b, (((1,), (1,)), ((), ())), preferred_element_type=F32)


def _dot_tn(a, b):
    return lax.dot_general(a, b, (((0,), (0,)), ((), ())), preferred_element_type=F32)


def _split3(x):
    h = x.astype(BF16)
    r = x - h.astype(F32)
    m = r.astype(BF16)
    l = (r - m.astype(F32)).astype(BF16)
    return h, m, l


def _mask_dot(mask_bf16, x):
    h, m, l = _split3(x)
    return _dot(mask_bf16, h) + _dot(mask_bf16, m) + _dot(mask_bf16, l)


def _dot_mask(x, mask_bf16):
    h, m, l = _split3(x)
    return _dot(h, mask_bf16) + _dot(m, mask_bf16) + _dot(l, mask_bf16)


def _sigmoid(x):
    return 1.0 / (1.0 + jnp.exp(-x))


def _log_sigmoid(x):
    return jnp.minimum(x, 0.0) - jnp.log1p(jnp.exp(-jnp.abs(x)))


def _mm_kernel(x_ref, w_ref, b_ref, o_ref, *, pre, act):
    x = x_ref[...]
    if pre == "silu":
        xf = x.astype(F32)
        x = (xf * _sigmoid(xf)).astype(BF16)
    acc = _dot(x, w_ref[...]) + b_ref[...]
    if act == "sigmoid":
        acc = _sigmoid(acc)
    o_ref[...] = acc.astype(o_ref.dtype)


def _mm(x, w, bias=None, *, tm, tn, pre=None, act=None, out_dtype=F32, name="mm"):
    M, K = x.shape
    N = w.shape[1]
    assert M % tm == 0 and N % tn == 0, (M, tm, N, tn)
    if bias is None:
        bias = jnp.zeros((1, N), F32)
    return pl.pallas_call(
        functools.partial(_mm_kernel, pre=pre, act=act),
        grid=(M // tm, N // tn),
        in_specs=[pl.BlockSpec((tm, K), lambda i, j: (i, 0)),
                  pl.BlockSpec((K, tn), lambda i, j: (0, j)),
                  pl.BlockSpec((1, tn), lambda i, j: (0, j))],
        out_specs=pl.BlockSpec((tm, tn), lambda i, j: (i, j)),
        out_shape=jax.ShapeDtypeStruct((M, N), out_dtype),
        compiler_params=_cparams(("parallel", "parallel")),
        name=name,
    )(x, w, bias)


def _norm_mod_kernel(x_ref, g_ref, shift_ref, scale_ref, o_ref):
    x = x_ref[...]
    y = x * lax.rsqrt(jnp.mean(x * x, axis=-1, keepdims=True) + NORM_EPS)
    o_ref[...] = ((y * g_ref[...]) * (1.0 + scale_ref[...]) + shift_ref[...]).astype(o_ref.dtype)


def _norm_mod(x, g, mod, shift_idx, scale_idx, *, ts):
    Bm, Sm, D = x.shape
    return pl.pallas_call(
        _norm_mod_kernel,
        grid=(Bm, Sm // ts),
        in_specs=[pl.BlockSpec((None, ts, D), lambda b, s: (b, s, 0)),
                  pl.BlockSpec((1, D), lambda b, s: (0, 0)),
                  pl.BlockSpec((None, 1, D), lambda b, s: (b, 0, shift_idx)),
                  pl.BlockSpec((None, 1, D), lambda b, s: (b, 0, scale_idx))],
        out_specs=pl.BlockSpec((None, ts, D), lambda b, s: (b, s, 0)),
        out_shape=jax.ShapeDtypeStruct((Bm, Sm, D), BF16),
        compiler_params=_cparams(("parallel", "parallel")),
        name="norm_mod",
    )(x, g, mod, mod)


def _mlp_kernel(h_ref, x_ref, gate_ref, w1_ref, w2_ref, o_ref, acc_ref):
    j = pl.program_id(2)
    a = jnp.maximum(_dot(h_ref[...], w1_ref[...]), 0.0)
    part = _dot((a * a).astype(BF16), w2_ref[...])

    @pl.when(j == 0)
    def _():
        acc_ref[...] = part

    @pl.when(j > 0)
    def _():
        acc_ref[...] += part

    @pl.when(j == pl.num_programs(2) - 1)
    def _():
        o_ref[...] = x_ref[...] + gate_ref[...] * acc_ref[...]


def _mlp(h, x, mod, w1, w2, *, ts, tf):
    Bm, Sm, D = x.shape
    FF = w1.shape[1]
    return pl.pallas_call(
        _mlp_kernel,
        grid=(Bm, Sm // ts, FF // tf),
        in_specs=[pl.BlockSpec((None, ts, D), lambda b, s, j: (b, s, 0)),
                  pl.BlockSpec((None, ts, D), lambda b, s, j: (b, s, 0)),
                  pl.BlockSpec((None, 1, D), lambda b, s, j: (b, 0, 5)),
                  pl.BlockSpec((D, tf), lambda b, s, j: (0, j)),
                  pl.BlockSpec((tf, D), lambda b, s, j: (j, 0))],
        out_specs=pl.BlockSpec((None, ts, D), lambda b, s, j: (b, s, 0)),
        out_shape=jax.ShapeDtypeStruct((Bm, Sm, D), F32),
        scratch_shapes=[pltpu.VMEM((ts, D), F32)],
        compiler_params=_cparams(("parallel", "parallel", "arbitrary")),
        name="mlp",
    )(h, x, mod, w1, w2)


def _merge_kernel(h_ref, ya_ref, yb_ref, yc_ref, yd_ref, x_ref, gate_ref, wg_ref, wo_ref, wout_ref,
                  o_ref, acc_ref):
    i = pl.program_id(2)
    g = _sigmoid(_dot(h_ref[...], wg_ref[...]))
    for idx, y_ref in enumerate((ya_ref, yb_ref, yc_ref, yd_ref)):
        @pl.when(i == idx)
        def _(idx=idx, y_ref=y_ref):
            contrib = g * _dot(y_ref[...], wo_ref[0:y_ref.shape[-1], :])
            if idx == 0:
                acc_ref[...] = contrib
            else:
                acc_ref[...] += contrib

    @pl.when(i == N_BRANCH - 1)
    def _():
        o_ref[...] = x_ref[...] + gate_ref[...] * _dot(acc_ref[...].astype(BF16), wout_ref[...])


def _merge(h, ya, yb, yc, yd, x, mod, wg, wo, wout, *, ts):
    Bm, Sm, D = x.shape

    def tok(width):
        return pl.BlockSpec((None, ts, width), lambda b, s, i: (b, s, 0))

    return pl.pallas_call(
        _merge_kernel,
        grid=(Bm, Sm // ts, N_BRANCH),
        in_specs=[tok(D), tok(ya.shape[-1]), tok(yb.shape[-1]), tok(yc.shape[-1]), tok(yd.shape[-1]), tok(D),
                  pl.BlockSpec((None, 1, D), lambda b, s, i: (b, 0, 2)),
                  pl.BlockSpec((None, D, D), lambda b, s, i: (i, 0, 0)),
                  pl.BlockSpec((None, D, D), lambda b, s, i: (i, 0, 0)),
                  pl.BlockSpec((D, D), lambda b, s, i: (0, 0))],
        out_specs=tok(D),
        out_shape=jax.ShapeDtypeStruct((Bm, Sm, D), F32),
        scratch_shapes=[pltpu.VMEM((ts, D), F32)],
        compiler_params=_cparams(("parallel", "parallel", "arbitrary")),
        name="merge",
    )(h, ya, yb, yc, yd, x, mod, wg, wo, wout)


def _rope_tables(n_tokens, start, rot):
    q = rot // 4
    t = jnp.arange(n_tokens, dtype=jnp.int32)
    pos_r = (t // GRID_W).astype(F32)
    pos_c = (t % GRID_W).astype(F32)
    inv = ROPE_BASE ** (-jnp.arange(q, dtype=F32) / q)
    ar = pos_r[:, None] * inv
    ac = pos_c[:, None] * inv
    zeros = jnp.zeros_like(ar)
    cos_t = jnp.concatenate([jnp.cos(ar), jnp.cos(ar), jnp.cos(ac), jnp.cos(ac)], -1)
    sin_prev = jnp.concatenate([zeros, jnp.sin(ar), zeros, jnp.sin(ac)], -1)
    sin_next = jnp.concatenate([-jnp.sin(ar), zeros, -jnp.sin(ac), zeros], -1)

    def place(tab, fill):
        left = jnp.full((n_tokens, start), fill, F32)
        right = jnp.full((n_tokens, LANES - start - rot), fill, F32)
        return jnp.concatenate([left, tab, right], -1)

    return place(cos_t, 1.0), place(sin_prev, 0.0), place(sin_next, 0.0)


def _rope(x, cos_t, sin_prev, sin_next, q):
    return (x * cos_t + pltpu.roll(x, q, 1) * sin_prev + pltpu.roll(x, LANES - q, 1) * sin_next)


def _mla_q_kernel(u_ref, qan_ref, wq_ref, qn_ref, cos_ref, sp_ref, sn_ref, q_ref, *, rope):
    qa = u_ref[...]
    qlat = (qa * lax.rsqrt(jnp.mean(qa * qa, axis=-1, keepdims=True) + NORM_EPS) * qan_ref[...]).astype(BF16)
    for h in range(MLA_HEADS):
        q = _dot(qlat, wq_ref[h])
        q = q * lax.rsqrt(jnp.sum(q * q, axis=-1, keepdims=True) * (1.0 / MLA_QK) + NORM_EPS) * qn_ref[...]
        if rope:
            q = _rope(q, cos_ref[...], sp_ref[...], sn_ref[...], MLA_ROPE // 4)
        q_ref[h] = (q * (MLA_SCALE * LOG2E)).astype(BF16)


def _mla_q(uA, qan, wq, qn, tabs, *, ts, rope):
    B, S, _ = uA.shape
    tab_spec = pl.BlockSpec((ts, LANES), lambda b, s: (s, 0))
    return pl.pallas_call(
        functools.partial(_mla_q_kernel, rope=rope),
        grid=(B, S // ts),
        in_specs=[pl.BlockSpec((None, ts, MLA_Q_RANK), lambda b, s: (b, s, 0)),
                  pl.BlockSpec((1, MLA_Q_RANK), lambda b, s: (0, 0)),
                  pl.BlockSpec((MLA_HEADS, MLA_Q_RANK, LANES), lambda b, s: (0, 0, 0)),
                  pl.BlockSpec((1, LANES), lambda b, s: (0, 0)),
                  tab_spec, tab_spec, tab_spec],
        out_specs=pl.BlockSpec((None, MLA_HEADS, ts, LANES), lambda b, s: (b, 0, s, 0)),
        out_shape=jax.ShapeDtypeStruct((B, MLA_HEADS, S, LANES), BF16),
        compiler_params=_cparams(("parallel", "parallel")),
        name="mla_q",
    )(uA, qan, wq, qn, *tabs)


def _with_ones(v):
    lane = lax.broadcasted_iota(jnp.int32, v.shape, 1)
    return jnp.where(lane == V_HD, 1.0, v)


def _mla_kv_kernel(kva_ref, kr_ref, kvn_ref, wk_ref, wv_ref, kn_ref, cos_ref, sp_ref, sn_ref,
                   k_ref, v_ref, ckv_ref, *, rope, normalise):
    c = kva_ref[...]
    if normalise:
        c = c * lax.rsqrt(jnp.mean(c * c, axis=-1, keepdims=True) + NORM_EPS) * kvn_ref[...]
    ckv_ref[...] = c
    cb = c.astype(BF16)
    for h in range(MLA_HEADS):
        k = _dot(cb, wk_ref[h]) + kr_ref[...]
        k = k * lax.rsqrt(jnp.sum(k * k, axis=-1, keepdims=True) * (1.0 / MLA_QK) + NORM_EPS) * kn_ref[...]
        if rope:
            k = _rope(k, cos_ref[...], sp_ref[...], sn_ref[...], MLA_ROPE // 4)
        k_ref[h] = k.astype(BF16)
        v_ref[h] = _with_ones(_dot(cb, wv_ref[h])).astype(BF16)


def _mla_kv(kva_src, kva_blk, kr_src, kr_blk, kvn, wk, wv, kn, tabs, *, ts, rope, normalise):
    B, S, _ = kva_src.shape
    tab_spec = pl.BlockSpec((ts, LANES), lambda b, s: (s, 0))
    head_out = pl.BlockSpec((None, MLA_HEADS, ts, LANES), lambda b, s: (b, 0, s, 0))
    w_spec = pl.BlockSpec((MLA_HEADS, LANES, LANES), lambda b, s: (0, 0, 0))
    return pl.pallas_call(
        functools.partial(_mla_kv_kernel, rope=rope, normalise=normalise),
        grid=(B, S // ts),
        in_specs=[pl.BlockSpec((None, ts, LANES), lambda b, s: (b, s, kva_blk)),
                  pl.BlockSpec((None, ts, LANES), lambda b, s: (b, s, kr_blk)),
                  pl.BlockSpec((1, LANES), lambda b, s: (0, 0)),
                  w_spec, w_spec,
                  pl.BlockSpec((1, LANES), lambda b, s: (0, 0)),
                  tab_spec, tab_spec, tab_spec],
        out_specs=[head_out, head_out, pl.BlockSpec((None, ts, LANES), lambda b, s: (b, s, 0))],
        out_shape=[jax.ShapeDtypeStruct((B, MLA_HEADS, S, LANES), BF16),
                   jax.ShapeDtypeStruct((B, MLA_HEADS, S, LANES), BF16),
                   jax.ShapeDtypeStruct((B, S, LANES), F32)],
        compiler_params=_cparams(("parallel", "parallel")),
        name="mla_kv",
    )(kva_src, kr_src, kvn, wk, wv, kn, *tabs)


def _attn_kernel(q_ref, k_ref, v_ref, sink_ref, o_ref, *, use_sink, n_sub):
    sub = q_ref.shape[0] // n_sub
    k = k_ref[...]
    v = v_ref[...]
    subs = range(n_sub)
    s = [_dot_nt(q_ref[j * sub:(j + 1) * sub, :], k) for j in subs]
    m = [jnp.max(s[j], axis=-1, keepdims=True) for j in subs]
    if use_sink:
        snk = sink_ref[...][:, 0:1] * LOG2E
        m = [jnp.maximum(m[j], snk) for j in subs]
    o = [_dot(jnp.exp2(s[j] - m[j]).astype(BF16), v) for j in subs]
    for j in subs:
        l = o[j][:, V_HD:V_HD + 1]
        if use_sink:
            l = l + jnp.exp2(snk - m[j])
        o_ref[j * sub:(j + 1) * sub, :] = (o[j] / l).astype(o_ref.dtype)


def _attn(q, k, v, sink, *, tq, use_sink, n_sub=1):
    B, H, Sq, _ = q.shape
    G, Sk = k.shape[1], k.shape[2]
    rep = H // G
    return pl.pallas_call(
        functools.partial(_attn_kernel, use_sink=use_sink, n_sub=n_sub),
        grid=(B, H, Sq // tq),
        in_specs=[pl.BlockSpec((None, None, tq, LANES), lambda b, h, i: (b, h, i, 0)),
                  pl.BlockSpec((None, None, Sk, LANES), lambda b, h, i: (b, h // rep, 0, 0)),
                  pl.BlockSpec((None, None, Sk, LANES), lambda b, h, i: (b, h // rep, 0, 0)),
                  pl.BlockSpec((None, 1, LANES), lambda b, h, i: (h, 0, 0))],
        out_specs=pl.BlockSpec((None, tq, LANES), lambda b, h, i: (b, i, h)),
        out_shape=jax.ShapeDtypeStruct((B, Sq, H * LANES), BF16),
        compiler_params=_cparams(("parallel", "parallel", "parallel")),
        name="attn",
    )(q, k, v, sink)


def _win_attn_kernel(q_ref, kp_ref, kc_ref, kn_ref, vp_ref, vc_ref, vn_ref, kx_ref, vx_ref, sink_ref, o_ref):
    i = pl.program_id(2)
    nb = pl.num_programs(2)
    blk = SW_BLOCK
    kk = jnp.concatenate([kp_ref[...], kc_ref[...], kn_ref[...], kx_ref[...]], axis=0)
    vv = jnp.concatenate([vp_ref[...], vc_ref[...], vn_ref[...], vx_ref[...]], axis=0)
    nk = kk.shape[0]
    qi = lax.broadcasted_iota(jnp.int32, (blk, nk), 0)
    kj = lax.broadcasted_iota(jnp.int32, (blk, nk), 1)
    first = (i == 0).astype(jnp.int32)
    last = (i == nb - 1).astype(jnp.int32)
    lo = qi * (1 - first) + blk * first
    hi = (qi + 2 * blk) * (1 - last) + (2 * blk - 1) * last
    mask = (kj >= lo) & ((kj <= hi) | (kj >= 3 * blk))
    heads = range(SW_REP)
    s = [jnp.where(mask, _dot_nt(q_ref[r], kk), NEG_BIG) for r in heads]
    snk = [sink_ref[r][:, 0:1] * LOG2E for r in heads]
    m = [jnp.maximum(jnp.max(s[r], axis=-1, keepdims=True), snk[r]) for r in heads]
    o = [_dot(jnp.exp2(s[r] - m[r]).astype(BF16), vv) for r in heads]
    for r in heads:
        l = o[r][:, V_HD:V_HD + 1] + jnp.exp2(snk[r] - m[r])
        o_ref[:, r * LANES:(r + 1) * LANES] = (o[r] / l).astype(o_ref.dtype)


def _win_attn(q, k, v, kx, vx, sink):
    B, H, S, _ = q.shape
    G = k.shape[1]
    C = kx.shape[2]
    nb = S // SW_BLOCK
    blk = SW_BLOCK

    def kv_spec(off):
        return pl.BlockSpec((None, None, blk, LANES),
                            lambda b, g, i: (b, g, jnp.clip(i + off, 0, nb - 1), 0))

    ctx_spec = pl.BlockSpec((None, None, C, LANES), lambda b, g, i: (b, g, 0, 0))
    return pl.pallas_call(
        _win_attn_kernel,
        grid=(B, G, nb),
        in_specs=[pl.BlockSpec((None, SW_REP, blk, LANES), lambda b, g, i: (b, g, i, 0)),
                  kv_spec(-1), kv_spec(0), kv_spec(1), kv_spec(-1), kv_spec(0), kv_spec(1),
                  ctx_spec, ctx_spec,
                  pl.BlockSpec((None, SW_REP, 1, LANES), lambda b, g, i: (g, 0, 0, 0))],
        out_specs=pl.BlockSpec((None, blk, SW_REP * LANES), lambda b, g, i: (b, i, g)),
        out_shape=jax.ShapeDtypeStruct((B, S, H * LANES), BF16),
        compiler_params=_cparams(("parallel", "parallel", "parallel")),
        name="win_attn",
    )(q, k, k, k, v, v, v, kx, vx, sink)


def _swa_prep_kernel(u_ref, qn_ref, kn_ref, cos_ref, sp_ref, sn_ref, q_ref, k_ref, v_ref, kraw_ref, *, rope):
    def norm(x, g):
        return x * lax.rsqrt(jnp.sum(x * x, axis=-1, keepdims=True) * (1.0 / SW_HD) + NORM_EPS) * g

    def rot(x):
        if not rope:
            return x
        return _rope(x, cos_ref[...], sp_ref[...], sn_ref[...], SW_HD // 4)

    for h in range(SW_HEADS):
        x = norm(u_ref[:, h * LANES:(h + 1) * LANES], qn_ref[...])
        q_ref[h] = (rot(x) * (SW_SCALE * LOG2E)).astype(BF16)
    k0 = SW_HEADS * LANES
    v0 = k0 + SW_KV_HEADS * LANES
    for g in range(SW_KV_HEADS):
        x = norm(u_ref[:, k0 + g * LANES:k0 + (g + 1) * LANES], kn_ref[...])
        kraw_ref[:, g * LANES:(g + 1) * LANES] = x
        k_ref[g] = rot(x).astype(BF16)
        v_ref[g] = _with_ones(u_ref[:, v0 + g * LANES:v0 + (g + 1) * LANES]).astype(BF16)


def _swa_prep(uC, qn, kn, tabs, *, ts, rope):
    B, S, _ = uC.shape
    tab_spec = pl.BlockSpec((ts, LANES), lambda b, s: (s, 0))
    v_spec = pl.BlockSpec((None, SW_KV_HEADS, ts, LANES), lambda b, s: (b, 0, s, 0))
    v_shape = jax.ShapeDtypeStruct((B, SW_KV_HEADS, S, LANES), BF16)
    return pl.pallas_call(
        functools.partial(_swa_prep_kernel, rope=rope),
        grid=(B, S // ts),
        in_specs=[pl.BlockSpec((None, ts, WC), lambda b, s: (b, s, 0)),
                  pl.BlockSpec((1, LANES), lambda b, s: (0, 0)),
                  pl.BlockSpec((1, LANES), lambda b, s: (0, 0)),
                  tab_spec, tab_spec, tab_spec],
        out_specs=[pl.BlockSpec((None, SW_HEADS, ts, LANES), lambda b, s: (b, 0, s, 0)),
                   pl.BlockSpec((None, SW_KV_HEADS, ts, LANES), lambda b, s: (b, 0, s, 0)),
                   v_spec,
                   pl.BlockSpec((None, ts, SW_KV_HEADS * LANES), lambda b, s: (b, s, 0))],
        out_shape=[jax.ShapeDtypeStruct((B, SW_HEADS, S, LANES), BF16),
                   jax.ShapeDtypeStruct((B, SW_KV_HEADS, S, LANES), BF16),
                   v_shape,
                   jax.ShapeDtypeStruct((B, S, SW_KV_HEADS * LANES), F32)],
        compiler_params=_cparams(("parallel", "parallel")),
        name="swa_prep",
    )(uC, qn, kn, *tabs)


def _dir_masks(L, d):
    row = lax.broadcasted_iota(jnp.int32, (L, L), 0)
    col = lax.broadcasted_iota(jnp.int32, (L, L), 1)
    if d == 0:
        return col <= row, col < row
    return col >= row, col > row


def _mlstm_kernel(qkvf_ref, qkvb_ref, gcf_ref, gcb_ref, grf_ref, grb_ref, bc_ref, br_ref, C0_ref, n0_ref, m0_ref,
                  hf_ref, hb_ref, Cout_ref, nout_ref, mout_ref, C_s, n_s, m_s):
    c = pl.program_id(1)
    L = ML_CHUNK
    H = ML_HEADS

    @pl.when(c == 0)
    def _():
        C_s[...] = C0_ref[...]
        n_s[...] = n0_ref[...]
        m_s[...] = m0_ref[...]

    chains = []
    for d, (qkv_ref, gc_ref, gr_ref, h_ref) in enumerate(((qkvf_ref, gcf_ref, grf_ref, hf_ref),
                                                          (qkvb_ref, gcb_ref, grb_ref, hb_ref))):
        incl, _ = _dir_masks(L, d)
        incl_t, _ = _dir_masks(L, 1 - d)
        incl_b = jnp.where(incl, 1.0, 0.0).astype(BF16)
        incl_t_b = jnp.where(incl_t, 1.0, 0.0).astype(BF16)
        gcol = gc_ref[...] + bc_ref[d]
        grow = gr_ref[...] + br_ref[d]
        lane = lax.broadcasted_iota(jnp.int32, gcol.shape, 1)
        gcol = jnp.where(lane < H, gcol, _log_sigmoid(gcol))
        rowi = lax.broadcasted_iota(jnp.int32, grow.shape, 0)
        grow = jnp.where(rowi < H, grow, _log_sigmoid(grow))
        bcol = _mask_dot(incl_b, gcol)
        brow = _dot_mask(grow, incl_t_b)
        tot = jnp.sum(gcol, axis=0, keepdims=True)
        for hh in range(H):
            ch = dict(d=d, hh=hh, h_ref=h_ref, incl=incl)
            ch["q"] = qkv_ref[:, hh * LANES:(hh + 1) * LANES] * (ML_DK ** -0.5)
            ch["k"] = qkv_ref[:, (H + hh) * LANES:(H + hh + 1) * LANES]
            ch["vb"] = qkv_ref[:, (2 * H + hh) * LANES:(2 * H + hh + 1) * LANES].astype(BF16)
            ch["qb"] = ch["q"].astype(BF16)
            ch["kb"] = ch["k"].astype(BF16)
            ch["i_col"] = gcol[:, hh:hh + 1]
            ch["b_col"] = bcol[:, H + hh:H + hh + 1]
            ch["i_row"] = grow[hh:hh + 1, :]
            ch["b_row"] = brow[H + hh:H + hh + 1, :]
            ch["bL"] = tot[:, H + hh:H + hh + 1]
            ch["C"] = C_s[d, hh]
            ch["n"] = n_s[d, hh]
            ch["m"] = m_s[d, hh][:, 0:1]
            chains.append(ch)

    for ch in chains:
        ch["S"] = _dot_nt(ch["qb"], ch["kb"])
        ch["qC"] = _dot(ch["qb"], ch["C"].astype(BF16))
    for ch in chains:
        D = jnp.where(ch["incl"], ch["b_col"] - ch["b_row"] + ch["i_row"], -jnp.inf)
        inter = ch["b_col"] + ch["m"]
        m_t = jnp.maximum(inter, jnp.max(D, axis=-1, keepdims=True))
        ch["W"] = ch["S"] * jnp.exp(D - m_t)
        ch["a_in"] = jnp.exp(inter - m_t)
        ch["m_t"] = m_t
        g_row = ch["bL"] - ch["b_row"] + ch["i_row"]
        m_new = jnp.maximum(ch["bL"] + ch["m"], jnp.max(g_row, axis=-1, keepdims=True))
        wk = jnp.exp(ch["bL"] - ch["b_col"] + ch["i_col"] - m_new)
        ch["decay"] = jnp.exp(ch["bL"] + ch["m"] - m_new)
        ch["m_new"] = m_new
        ch["ks"] = ch["k"] * wk
    for ch in chains:
        ch["Wv"] = _dot(ch["W"].astype(BF16), ch["vb"])
        ch["kv"] = _dot_tn(ch["ks"].astype(BF16), ch["vb"])
    for ch in chains:
        d, hh = ch["d"], ch["hh"]
        num = ch["Wv"] + ch["a_in"] * ch["qC"]
        den = (jnp.sum(ch["W"], axis=-1, keepdims=True)
               + ch["a_in"] * jnp.sum(ch["q"] * ch["n"], axis=-1, keepdims=True))
        ch["h_ref"][:, hh * LANES:(hh + 1) * LANES] = num / jnp.maximum(jnp.abs(den), jnp.exp(-ch["m_t"]))
        C_s[d, hh] = ch["decay"] * ch["C"] + ch["kv"]
        n_s[d, hh] = ch["decay"] * ch["n"] + jnp.sum(ch["ks"], axis=0, keepdims=True)
        m_s[d, hh] = jnp.broadcast_to(ch["m_new"], (1, LANES))

    @pl.when(c == pl.num_programs(1) - 1)
    def _():
        Cout_ref[...] = C_s[...]
        nout_ref[...] = n_s[...]
        mout_ref[...] = m_s[...]


def _mlstm(uB, g_rows, bias_col, bias_row, C0, n0, m0):
    B, S, _ = uB.shape
    L, H = ML_CHUNK, ML_HEADS
    nc = S // L
    gate_blk = 4 * 512 // LANES
    rev = lambda c: nc - 1 - c

    st_spec = lambda shape: pl.BlockSpec((None, 2) + shape, lambda b, c: (b, 0) + (0,) * len(shape))
    return pl.pallas_call(
        _mlstm_kernel,
        grid=(B, nc),
        in_specs=[pl.BlockSpec((None, L, 3 * 512), lambda b, c: (b, c, 0)),
                  pl.BlockSpec((None, L, 3 * 512), lambda b, c: (b, rev(c), 0)),
                  pl.BlockSpec((None, L, LANES), lambda b, c: (b, c, gate_blk)),
                  pl.BlockSpec((None, L, LANES), lambda b, c: (b, rev(c), gate_blk + 1)),
                  pl.BlockSpec((None, None, None, 2 * H, L), lambda b, c: (b, 0, c, 0, 0)),
                  pl.BlockSpec((None, None, None, 2 * H, L), lambda b, c: (b, 1, rev(c), 0, 0)),
                  pl.BlockSpec((2, 1, LANES), lambda b, c: (0, 0, 0)),
                  pl.BlockSpec((2, 2 * H, L), lambda b, c: (0, 0, 0)),
                  st_spec((H, LANES, LANES)), st_spec((H, 1, LANES)), st_spec((H, 1, LANES))],
        out_specs=[pl.BlockSpec((None, L, H * LANES), lambda b, c: (b, c, 0)),
                   pl.BlockSpec((None, L, H * LANES), lambda b, c: (b, rev(c), 0)),
                   st_spec((H, LANES, LANES)), st_spec((H, 1, LANES)), st_spec((H, 1, LANES))],
        out_shape=[jax.ShapeDtypeStruct((B, S, H * LANES), F32),
                   jax.ShapeDtypeStruct((B, S, H * LANES), F32),
                   jax.ShapeDtypeStruct((B, 2, H, LANES, LANES), F32),
                   jax.ShapeDtypeStruct((B, 2, H, 1, LANES), F32),
                   jax.ShapeDtypeStruct((B, 2, H, 1, LANES), F32)],
        scratch_shapes=[pltpu.VMEM((2, H, LANES, LANES), F32),
                        pltpu.VMEM((2, H, 1, LANES), F32),
                        pltpu.VMEM((2, H, 1, LANES), F32)],
        compiler_params=_cparams(("parallel", "arbitrary")),
        name="mlstm",
    )(uB, uB, uB, uB, g_rows, g_rows, bias_col, bias_row, C0, n0, m0)


def _mlstm_post_kernel(hf_ref, hb_ref, o_ref, g_ref, y_ref):
    for hh in range(ML_HEADS):
        sl = slice(hh * LANES, (hh + 1) * LANES)
        x = hf_ref[:, sl] + hb_ref[:, sl]
        y = x * lax.rsqrt(jnp.mean(x * x, axis=-1, keepdims=True) + NORM_EPS) * g_ref[...]
        y_ref[:, sl] = (y * _sigmoid(o_ref[:, sl])).astype(y_ref.dtype)


def _mlstm_post(h_f, h_b, uB, g, *, ts):
    B, S, W = h_f.shape
    return pl.pallas_call(
        _mlstm_post_kernel,
        grid=(B, S // ts),
        in_specs=[pl.BlockSpec((None, ts, W), lambda b, s: (b, s, 0)),
                  pl.BlockSpec((None, ts, W), lambda b, s: (b, s, 0)),
                  pl.BlockSpec((None, ts, W), lambda b, s: (b, s, 3)),
                  pl.BlockSpec((1, LANES), lambda b, s: (0, 0))],
        out_specs=pl.BlockSpec((None, ts, W), lambda b, s: (b, s, 0)),
        out_shape=jax.ShapeDtypeStruct((B, S, W), BF16),
        compiler_params=_cparams(("parallel", "parallel")),
        name="mlstm_post",
    )(h_f, h_b, uB, g)


def _rwkv_kernel(rkvf_ref, rkvb_ref, w1f_ref, w1b_ref, a1f_ref, a1b_ref, w2_ref, w0_ref, a2_ref, a0_ref,
                 kk_ref, ka_ref, S0_ref, yf_ref, yb_ref, Sout_ref, S_s):
    c = pl.program_id(1)
    T = RW_CHUNK
    HW = RW_HEADS * LANES

    @pl.when(c == 0)
    def _():
        S_s[...] = S0_ref[...]

    eye = (lax.broadcasted_iota(jnp.int32, (T, T), 0) == lax.broadcasted_iota(jnp.int32, (T, T), 1)).astype(F32)

    chains = []
    for d, (rkv_ref, w1_ref, a1_ref, y_ref) in enumerate(((rkvf_ref, w1f_ref, a1f_ref, yf_ref),
                                                          (rkvb_ref, w1b_ref, a1b_ref, yb_ref))):
        incl, strict = _dir_masks(T, d)
        incl_b = jnp.where(incl, 1.0, 0.0).astype(BF16)
        r = rkv_ref[:, 0:HW]
        k = rkv_ref[:, HW:2 * HW]
        v = rkv_ref[:, 2 * HW:3 * HW]
        w_pre = _dot(jnp.tanh(w1_ref[...]).astype(BF16), w2_ref[d]) + w0_ref[d]
        logw = -RW_DECAY_SCALE * _sigmoid(w_pre)
        a = _sigmoid(_dot(a1_ref[...].astype(BF16), a2_ref[d]) + a0_ref[d])
        kappa = k * kk_ref[...]
        kt = k * (1.0 + (a - 1.0) * ka_ref[...])
        cs = _mask_dot(incl_b, logw)
        tot = jnp.sum(logw, axis=0, keepdims=True)
        e_pos = jnp.exp(cs)
        e_neg = jnp.exp(-cs)
        e_prev = jnp.exp(cs - logw)
        e_tail = jnp.exp(tot - cs)
        g_tot = jnp.exp(tot)
        for hh in range(RW_HEADS):
            sl = slice(hh * LANES, (hh + 1) * LANES)
            kap = kappa[:, sl]
            kh = kap * lax.rsqrt(jnp.sum(kap * kap, axis=-1, keepdims=True) + 1e-12)
            bb = a[:, sl] * kh
            kth = kt[:, sl]
            chains.append(dict(
                d=d, hh=hh, sl=sl, y_ref=y_ref, incl=incl, strict=strict, g_tot=g_tot[:, sl],
                Q=(kh * e_prev[:, sl]).astype(BF16), Kh=(kth * e_neg[:, sl]).astype(BF16),
                Bh=(bb * e_neg[:, sl]).astype(BF16), R=(r[:, sl] * e_pos[:, sl]).astype(BF16),
                Kb=(kth * e_tail[:, sl]).astype(BF16), Bb=(bb * e_tail[:, sl]).astype(BF16),
                V=v[:, sl].astype(BF16)))

    for ch in chains:
        ch["N"] = jnp.where(ch["strict"], -_dot_nt(ch["Q"], ch["Bh"]), 0.0)
        ch["Lk"] = jnp.where(ch["strict"], _dot_nt(ch["Q"], ch["Kh"]), 0.0).astype(BF16)
        ch["Ak"] = jnp.where(ch["incl"], _dot_nt(ch["R"], ch["Kh"]), 0.0).astype(BF16)
        ch["Ab"] = jnp.where(ch["incl"], _dot_nt(ch["R"], ch["Bh"]), 0.0).astype(BF16)
        ch["X"] = eye + ch["N"]
        ch["P"] = ch["N"]
    for _ in range(5):
        for ch in chains:
            Pb = ch["P"].astype(BF16)
            ch["P"] = _dot(Pb, Pb)
        for ch in chains:
            ch["X"] = ch["X"] + _dot(ch["X"].astype(BF16), ch["P"].astype(BF16))
    for ch in chains:
        ch["S0"] = S_s[ch["d"], ch["hh"]]
        ch["S0b"] = ch["S0"].astype(BF16)
        ch["rhs"] = _dot_nt(ch["Q"], ch["S0b"]) + _dot(ch["Lk"], ch["V"])
        ch["YA"] = _dot_nt(ch["R"], ch["S0b"]) + _dot(ch["Ak"], ch["V"])
        ch["SA"] = ch["S0"] * ch["g_tot"] + _dot_tn(ch["V"], ch["Kb"])
    for ch in chains:
        ch["Zb"] = _dot(ch["X"].astype(BF16), ch["rhs"].astype(BF16)).astype(BF16)
    for ch in chains:
        ch["y_ref"][:, ch["sl"]] = ch["YA"] - _dot(ch["Ab"], ch["Zb"])
        S_s[ch["d"], ch["hh"]] = ch["SA"] - _dot_tn(ch["Zb"], ch["Bb"])

    @pl.when(c == pl.num_programs(1) - 1)
    def _():
        Sout_ref[...] = S_s[...]


def _rwkv(uD, w2, w0, a2, a0, kk, ka, S0):
    B, S, _ = uD.shape
    T = RW_CHUNK
    nc = S // T
    HW = RW_HEADS * LANES
    rev = lambda c: nc - 1 - c
    full = lambda shape: pl.BlockSpec(shape, lambda b, c: (0,) * len(shape))
    st_spec = pl.BlockSpec((None, 2, RW_HEADS, LANES, LANES), lambda b, c: (b, 0, 0, 0, 0))
    return pl.pallas_call(
        _rwkv_kernel,
        grid=(B, nc),
        in_specs=[pl.BlockSpec((None, T, 3 * HW), lambda b, c: (b, c, 0)),
                  pl.BlockSpec((None, T, 3 * HW), lambda b, c: (b, rev(c), 0)),
                  pl.BlockSpec((None, T, LANES), lambda b, c: (b, c, RW_W1_BLK)),
                  pl.BlockSpec((None, T, LANES), lambda b, c: (b, rev(c), RW_W1_BLK + 1)),
                  pl.BlockSpec((None, T, LANES), lambda b, c: (b, c, RW_A1_BLK)),
                  pl.BlockSpec((None, T, LANES), lambda b, c: (b, rev(c), RW_A1_BLK + 1)),
                  full((2, LANES, HW)), full((2, 1, HW)), full((2, LANES, HW)), full((2, 1, HW)),
                  full((1, HW)), full((1, HW)),
                  st_spec],
        out_specs=[pl.BlockSpec((None, T, HW), lambda b, c: (b, c, 0)),
                   pl.BlockSpec((None, T, HW), lambda b, c: (b, rev(c), 0)),
                   st_spec],
        out_shape=[jax.ShapeDtypeStruct((B, S, HW), F32),
                   jax.ShapeDtypeStruct((B, S, HW), F32),
                   jax.ShapeDtypeStruct((B, 2, RW_HEADS, LANES, LANES), F32)],
        scratch_shapes=[pltpu.VMEM((2, RW_HEADS, LANES, LANES), F32)],
        compiler_params=_cparams(("parallel", "arbitrary")),
        name="rwkv",
    )(uD, uD, uD, uD, uD, uD, w2, w0, a2, a0, kk, ka, S0)


def _rwkv_post_kernel(yf_ref, yb_ref, u_ref, a2_ref, a0_ref, ka_ref, uu_ref, g2_ref, gng_ref, gnb_ref, o_ref):
    HW = RW_HEADS * LANES
    r = u_ref[:, 0:HW]
    k = u_ref[:, HW:2 * HW]
    v = u_ref[:, 2 * HW:3 * HW]
    a1_0 = RW_A1_BLK * LANES
    g1_0 = a1_0 + 2 * LANES
    g = _dot(_sigmoid(u_ref[:, g1_0:g1_0 + LANES]).astype(BF16), g2_ref[...])
    rk = []
    for dd in range(2):
        a1 = u_ref[:, a1_0 + dd * LANES:a1_0 + (dd + 1) * LANES]
        a = _sigmoid(_dot(a1.astype(BF16), a2_ref[dd]) + a0_ref[dd])
        kt = k * (1.0 + (a - 1.0) * ka_ref[...])
        rk.append(r * kt * uu_ref[dd])
    y = yf_ref[...] + yb_ref[...]
    lane = lax.broadcasted_iota(jnp.int32, (1, LANES), 1)
    real = lane < RW_N
    for hh in range(RW_HEADS):
        sl = slice(hh * LANES, (hh + 1) * LANES)
        yh = y[:, sl]
        mu = jnp.sum(yh, axis=-1, keepdims=True) * (1.0 / RW_N)
        dev = jnp.where(real, yh - mu, 0.0)
        var = jnp.sum(dev * dev, axis=-1, keepdims=True) * (1.0 / RW_N)
        yn = dev * lax.rsqrt(var + RW_GN_EPS) * gng_ref[:, sl] + gnb_ref[:, sl]
        bonus = (jnp.sum(rk[0][:, sl], axis=-1, keepdims=True)
                 + jnp.sum(rk[1][:, sl], axis=-1, keepdims=True)) * v[:, sl]
        o_ref[:, sl] = ((yn + bonus) * g[:, sl]).astype(o_ref.dtype)


def _rwkv_post(y_f, y_b, uD, a2, a0, ka, uu, g2, gng, gnb, *, ts):
    B, S, HW = y_f.shape
    full = lambda shape: pl.BlockSpec(shape, lambda b, s: (0,) * len(shape))
    return pl.pallas_call(
        _rwkv_post_kernel,
        grid=(B, S // ts),
        in_specs=[pl.BlockSpec((None, ts, HW), lambda b, s: (b, s, 0)),
                  pl.BlockSpec((None, ts, HW), lambda b, s: (b, s, 0)),
                  pl.BlockSpec((None, ts, WD), lambda b, s: (b, s, 0)),
                  full((2, LANES, HW)), full((2, 1, HW)), full((1, HW)), full((2, 1, HW)),
                  full((LANES, HW)), full((1, HW)), full((1, HW))],
        out_specs=pl.BlockSpec((None, ts, HW), lambda b, s: (b, s, 0)),
        out_shape=jax.ShapeDtypeStruct((B, S, HW), BF16),
        compiler_params=_cparams(("parallel", "parallel")),
        name="rwkv_post",
    )(y_f, y_b, uD, a2, a0, ka, uu, g2, gng, gnb)


def _pad_heads(w, nh, hd, axis=-1):
    axis = axis % w.ndim
    shp = w.shape
    w = w.reshape(shp[:axis] + (nh, hd) + shp[axis + 1:])
    pad = [(0, 0)] * w.ndim
    pad[axis + 1] = (0, LANES - hd)
    w = jnp.pad(w, pad)
    return w.reshape(shp[:axis] + (nh * LANES,) + shp[axis + 1:])


def _pad_cols(w, width):
    return jnp.pad(w, [(0, 0)] * (w.ndim - 1) + [(0, width - w.shape[-1])])


def _layer_weights(P, l):
    w_in = P["w_in"][l]
    col = lambda i: w_in[:, _IN_OFFS[i]:_IN_OFFS[i + 1]]
    zc = lambda n: jnp.zeros((D_MODEL, n), F32)
    W = {}
    W["wA"] = jnp.concatenate([col(0), col(1), zc(MLA_NOPE), col(2), zc(LANES - MLA_NOPE - MLA_ROPE)], 1).astype(BF16)
    gi, gf = col(6), col(7)
    gates = [jnp.concatenate([gi[:, d * ML_HEADS:(d + 1) * ML_HEADS], gf[:, d * ML_HEADS:(d + 1) * ML_HEADS],
                              zc(LANES - 2 * ML_HEADS)], 1) for d in range(2)]
    W["wB"] = jnp.concatenate([_pad_heads(col(3), ML_HEADS, ML_DK), _pad_heads(col(4), ML_HEADS, ML_DK),
                               col(5), col(8)] + gates, 1).astype(BF16)
    W["wC"] = jnp.concatenate([_pad_heads(col(9), SW_HEADS, SW_HD), _pad_heads(col(10), SW_KV_HEADS, SW_HD),
                               _pad_heads(col(11), SW_KV_HEADS, SW_HD)], 1).astype(BF16)
    W["wD"] = jnp.concatenate([_pad_heads(col(12), RW_HEADS, RW_N), _pad_heads(col(13), RW_HEADS, RW_N),
                               _pad_heads(col(14), RW_HEADS, RW_N),
                               _pad_heads(col(15), 2, RW_W_RANK), _pad_heads(col(16), 2, RW_A_RANK),
                               col(17), zc(LANES)], 1).astype(BF16)
    W["wG"] = jnp.transpose(col(18).reshape(D_MODEL, N_BRANCH, D_MODEL), (1, 0, 2)).astype(BF16)
    wo_a = _pad_heads(P["mla_w_o"][l], MLA_HEADS, MLA_V, axis=0)
    wo_b = jnp.pad(P["mlstm_w_o"][l], ((0, D_MODEL - ML_HEADS * ML_DV), (0, 0)))
    wo_c = _pad_heads(P["swa_w_o"][l], SW_HEADS, SW_HD, axis=0)
    wo_d = _pad_heads(P["rwkv_w_o"][l], RW_HEADS, RW_N, axis=0)
    W["wO"] = jnp.stack([wo_a, wo_b, wo_c, wo_d]).astype(BF16)
    W["w_out"] = P["w_out"][l].astype(BF16)
    W["mlp_w1"] = P["mlp_w1"][l].astype(BF16)
    W["mlp_w2"] = P["mlp_w2"][l].astype(BF16)
    W["norm1"] = P["norm1"][l][None, :]
    W["norm2"] = P["norm2"][l][None, :]
    W["q_a_norm"] = P["mla_q_a_norm"][l][None, :]
    W["kv_a_norm"] = P["mla_kv_a_norm"][l][None, :]
    wq = P["mla_w_uq"][l].reshape(MLA_Q_RANK, MLA_HEADS, MLA_QK)
    W["wq"] = jnp.transpose(_pad_cols(wq, LANES), (1, 0, 2)).astype(BF16)
    wkv = P["mla_w_ukv"][l].reshape(MLA_KV_RANK, MLA_HEADS, MLA_NOPE + MLA_V)
    W["wk"] = jnp.transpose(_pad_cols(wkv[..., :MLA_NOPE], LANES), (1, 0, 2)).astype(BF16)
    W["wv"] = jnp.transpose(_pad_cols(wkv[..., MLA_NOPE:], LANES), (1, 0, 2)).astype(BF16)
    W["q_norm"] = _pad_cols(P["mla_q_norm"][l][None, :], LANES)
    W["k_norm"] = _pad_cols(P["mla_k_norm"][l][None, :], LANES)
    ib, fb = P["mlstm_i_bias"][l], P["mlstm_f_bias"][l]
    bias = jnp.concatenate([ib, fb], -1)
    W["ml_bias_col"] = _pad_cols(bias, LANES)[:, None, :]
    W["ml_bias_row"] = jnp.broadcast_to(bias[:, :, None], (2, 2 * ML_HEADS, ML_CHUNK))
    W["ml_norm"] = P["mlstm_norm"][l][None, :]
    W["swa_q_norm"] = _pad_cols(P["swa_q_norm"][l][None, :], LANES)
    W["swa_k_norm"] = _pad_cols(P["swa_k_norm"][l][None, :], LANES)
    sink = jnp.broadcast_to(P["swa_sink"][l][:, None, None], (SW_HEADS, 1, LANES))
    W["sink"] = sink
    W["sink_g"] = sink.reshape(SW_KV_HEADS, SW_REP, 1, LANES)
    ph = lambda w: _pad_heads(w, RW_HEADS, RW_N)
    W["rw_w2"] = jnp.pad(ph(P["rwkv_w2"][l]), ((0, 0), (0, LANES - RW_W_RANK), (0, 0))).astype(BF16)
    W["rw_a2"] = jnp.pad(ph(P["rwkv_a2"][l]), ((0, 0), (0, LANES - RW_A_RANK), (0, 0))).astype(BF16)
    W["rw_w0"] = ph(P["rwkv_w0"][l])[:, None, :]
    W["rw_a0"] = ph(P["rwkv_a0"][l])[:, None, :]
    W["rw_g2"] = ph(P["rwkv_g2"][l]).astype(BF16)
    W["rw_kk"] = ph(P["rwkv_kk"][l])[None, :]
    W["rw_ka"] = ph(P["rwkv_ka"][l])[None, :]
    W["rw_u"] = ph(P["rwkv_u"][l])[:, None, :]
    W["rw_gn_g"] = ph(P["rwkv_gn_g"][l])[None, :]
    W["rw_gn_b"] = ph(P["rwkv_gn_b"][l])[None, :]
    return W


def _tile(n, pref):
    t = min(n, pref)
    assert n % t == 0, (n, t)
    return t


def _trunk_layer(x, mod, W, ctx, B, S):
    Bm, Sm, D = x.shape
    ntok = Bm * Sm
    ts = _tile(Sm, 1024)
    tm = _tile(ntok, 1024)
    tsq = _tile(S, 256)
    latent = ctx is not None

    h = _norm_mod(x, W["norm1"], mod, 0, 1, ts=ts)
    hf = h.reshape(ntok, D)
    uA = _mm(hf, W["wA"], tm=tm, tn=WA, name="proj_a").reshape(B, S, WA)
    uB = _mm(hf, W["wB"], tm=tm, tn=768, name="proj_b").reshape(B, S, WB)
    uC = _mm(hf, W["wC"], tm=tm, tn=768, name="proj_c").reshape(B, S, WC)
    uD = _mm(hf, W["wD"], tm=tm, tn=768, name="proj_d").reshape(B, S, WD)

    if latent:
        tabs = _rope_tables(S, MLA_NOPE, MLA_ROPE)
    else:
        tabs = tuple(jnp.zeros((S, LANES), F32) for _ in range(3))
    tsp = _tile(S, 512)
    q_mla = _mla_q(uA, W["q_a_norm"], W["wq"], W["q_norm"], tabs, ts=tsp, rope=latent)
    k_mla, v_mla, c_kv = _mla_kv(uA, MLA_Q_RANK // LANES, uA, MLA_Q_RANK // LANES + 1, W["kv_a_norm"],
                                 W["wk"], W["wv"], W["k_norm"], tabs, ts=tsp, rope=latent, normalise=True)
    if latent:
        C = ctx["mla_ckv"].shape[1]
        ctabs = tuple(jnp.zeros((C, LANES), F32) for _ in range(3))
        kr_c = jnp.pad(ctx["mla_krope"], ((0, 0), (0, 0), (MLA_NOPE, LANES - MLA_NOPE - MLA_ROPE)))
        k_c, v_c, _ = _mla_kv(ctx["mla_ckv"], 0, kr_c, 0, W["kv_a_norm"], W["wk"], W["wv"], W["k_norm"],
                              ctabs, ts=C, rope=False, normalise=False)
        k_mla = jnp.concatenate([k_mla, k_c], 2)
        v_mla = jnp.concatenate([v_mla, v_c], 2)
    y_a = _attn(q_mla, k_mla, v_mla, W["sink"], tq=_tile(S, 512), use_sink=False, n_sub=2 if latent else 1)

    L = ML_CHUNK
    nc = S // L
    g0 = 4 * 512
    g_rows = uB[:, :, g0:g0 + 2 * LANES].reshape(B, nc, L, 2, LANES)[..., :2 * ML_HEADS]
    g_rows = jnp.transpose(g_rows, (0, 3, 1, 4, 2))
    if latent:
        C0 = jnp.pad(ctx["mlstm_C"], ((0, 0), (0, 0), (0, 0), (0, LANES - ML_DK), (0, 0)))
        n0 = _pad_cols(ctx["mlstm_n"], LANES)[:, :, :, None, :]
        m0 = jnp.broadcast_to(ctx["mlstm_m"][..., None, None], (B, 2, ML_HEADS, 1, LANES))
    else:
        C0 = jnp.zeros((B, 2, ML_HEADS, LANES, LANES), F32)
        n0 = jnp.zeros((B, 2, ML_HEADS, 1, LANES), F32)
        m0 = jnp.zeros((B, 2, ML_HEADS, 1, LANES), F32)
    h_fwd, h_bwd, C_f, n_f, m_f = _mlstm(uB, g_rows, W["ml_bias_col"], W["ml_bias_row"], C0, n0, m0)
    y_b = _mlstm_post(h_fwd, h_bwd, uB, W["ml_norm"], ts=_tile(S, 512))

    if latent:
        stabs = _rope_tables(S, 0, SW_HD)
    else:
        stabs = tuple(jnp.zeros((S, LANES), F32) for _ in range(3))
    sq, sk, sv, sk_raw = _swa_prep(uC, W["swa_q_norm"], W["swa_k_norm"], stabs, ts=tsq, rope=latent)
    if latent:
        kx = _pad_cols(jnp.transpose(ctx["swa_k"], (0, 2, 1, 3)), LANES).astype(BF16)
        vx = jnp.transpose(ctx["swa_v"], (0, 2, 1, 3))
        vx = _pad_cols(jnp.concatenate([vx, jnp.ones(vx.shape[:-1] + (1,), F32)], -1), LANES).astype(BF16)
        y_c = _win_attn(sq, sk, sv, kx, vx, W["sink_g"])
    else:
        y_c = _attn(sq, sk, sv, W["sink"], tq=tsq, use_sink=True)

    if latent:
        S0 = jnp.pad(ctx["rwkv"], ((0, 0), (0, 0), (0, 0), (0, LANES - RW_N), (0, LANES - RW_N)))
    else:
        S0 = jnp.zeros((B, 2, RW_HEADS, LANES, LANES), F32)
    y_fwd, y_bwd, S_rw = _rwkv(uD, W["rw_w2"], W["rw_w0"], W["rw_a2"], W["rw_a0"], W["rw_kk"], W["rw_ka"], S0)
    y_d = _rwkv_post(y_fwd, y_bwd, uD, W["rw_a2"], W["rw_a0"], W["rw_ka"], W["rw_u"], W["rw_g2"],
                     W["rw_gn_g"], W["rw_gn_b"], ts=_tile(S, 256))

    tok = lambda y: y.reshape(Bm, Sm, y.shape[-1])
    x = _merge(h, tok(y_a), tok(y_b), tok(y_c), tok(y_d), x, mod, W["wG"], W["wO"], W["w_out"], ts=_tile(Sm, 512))
    h2 = _norm_mod(x, W["norm2"], mod, 3, 4, ts=ts)
    x = _mlp(h2, x, mod, W["mlp_w1"], W["mlp_w2"], ts=ts, tf=512)

    new_ctx = None
    if not latent:
        kr0 = MLA_Q_RANK + MLA_KV_RANK + MLA_NOPE
        v0 = (SW_HEADS + SW_KV_HEADS) * LANES
        new_ctx = {
            "mla_ckv": c_kv,
            "mla_krope": uA[:, :, kr0:kr0 + MLA_ROPE],
            "swa_k": sk_raw.reshape(B, S, SW_KV_HEADS, LANES)[..., :SW_HD],
            "swa_v": uC[:, :, v0:].reshape(B, S, SW_KV_HEADS, LANES)[..., :SW_HD],
            "mlstm_C": C_f[:, :, :, :ML_DK, :],
            "mlstm_n": n_f[:, :, :, 0, :ML_DK],
            "mlstm_m": m_f[:, :, :, 0, 0],
            "rwkv": S_rw[:, :, :, :RW_N, :RW_N],
        }
    return x, new_ctx


def kernel(x_prompt, x_sample, c, cache_mla_ckv, cache_mla_krope, cache_swa_k, cache_swa_v, state_mlstm_C, state_mlstm_n, state_mlstm_m, state_rwkv, c_ctx, ada_w, ada_b, norm1, norm2, w_in, mla_q_a_norm, mla_kv_a_norm, mla_w_uq, mla_w_ukv, mla_q_norm, mla_k_norm, mla_w_o, mlstm_i_bias, mlstm_f_bias, mlstm_norm, mlstm_w_o, swa_q_norm, swa_k_norm, swa_sink, swa_w_o, rwkv_w0, rwkv_w2, rwkv_a0, rwkv_a2, rwkv_g2, rwkv_kk, rwkv_ka, rwkv_u, rwkv_gn_g, rwkv_gn_b, rwkv_w_o, w_out, mlp_w1, mlp_w2):
    P = {'norm1': norm1, 'norm2': norm2, 'w_in': w_in,
         'mla_q_a_norm': mla_q_a_norm, 'mla_kv_a_norm': mla_kv_a_norm, 'mla_w_uq': mla_w_uq,
         'mla_w_ukv': mla_w_ukv, 'mla_q_norm': mla_q_norm, 'mla_k_norm': mla_k_norm, 'mla_w_o': mla_w_o,
         'mlstm_i_bias': mlstm_i_bias, 'mlstm_f_bias': mlstm_f_bias, 'mlstm_norm': mlstm_norm, 'mlstm_w_o': mlstm_w_o,
         'swa_q_norm': swa_q_norm, 'swa_k_norm': swa_k_norm, 'swa_sink': swa_sink, 'swa_w_o': swa_w_o,
         'rwkv_w0': rwkv_w0, 'rwkv_w2': rwkv_w2, 'rwkv_a0': rwkv_a0, 'rwkv_a2': rwkv_a2, 'rwkv_g2': rwkv_g2,
         'rwkv_kk': rwkv_kk, 'rwkv_ka': rwkv_ka, 'rwkv_u': rwkv_u, 'rwkv_gn_g': rwkv_gn_g, 'rwkv_gn_b': rwkv_gn_b,
         'rwkv_w_o': rwkv_w_o, 'w_out': w_out, 'mlp_w1': mlp_w1, 'mlp_w2': mlp_w2}
    depth = w_in.shape[0]
    Bc, Sc, D = x_prompt.shape
    Bl, Sl, _ = x_sample.shape

    cond = jnp.concatenate([c, c_ctx[None, :]], 0)
    rows = cond.shape[0]
    rows_pad = -(-rows // 16) * 16
    cond = jnp.pad(cond, ((0, rows_pad - rows), (0, 0)))

    xc = x_prompt.reshape(1, Bc * Sc, D)
    xl = x_sample
    states = []
    for l in range(depth):
        W = _layer_weights(P, l)
        mod = _mm(cond, ada_w[l].astype(BF16), ada_b[l][None, :], tm=rows_pad, tn=1024, pre="silu", name="ada")
        mod_l = mod[:Bl, None, :]
        mod_c = mod[Bl:Bl + 1, None, :]
        xc, st = _trunk_layer(xc, mod_c, W, None, Bc, Sc)
        states.append(st)
        ctx = {'mla_ckv': cache_mla_ckv[:, l], 'mla_krope': cache_mla_krope[:, l],
               'swa_k': cache_swa_k[:, l], 'swa_v': cache_swa_v[:, l],
               'mlstm_C': state_mlstm_C[:, l], 'mlstm_n': state_mlstm_n[:, l], 'mlstm_m': state_mlstm_m[:, l],
               'rwkv': state_rwkv[:, l]}
        xl, _ = _trunk_layer(xl, mod_l, W, ctx, Bl, Sl)

    y_prompt = xc.reshape(Bc, Sc, D)
    stack = lambda name: jnp.stack([s[name] for s in states], 1)
    return (y_prompt, xl, stack('mla_ckv'), stack('mla_krope'), stack('swa_k'), stack('swa_v'),
            stack('mlstm_C'), stack('mlstm_n'), stack('mlstm_m'), stack('rwkv'))
```

```python
import functools

import jax
import jax.numpy as jnp
import numpy as np
from jax import lax
from jax.experimental import pallas as pl
from jax.experimental.pallas import tpu as pltpu

F32 = jnp.float32
BF16 = jnp.bfloat16

LANES = 128
VMEM_LIMIT_BYTES = 56 * 1024 * 1024

D_MODEL = 1024
NORM_EPS = 1e-6
ROPE_BASE = 10000.0
GRID_W = 64
NEG_BIG = -1e30
N_MOD = 6
N_BRANCH = 4
D_FF = 4 * D_MODEL

MLA_HEADS, MLA_NOPE, MLA_ROPE, MLA_V = 8, 64, 32, 64
MLA_QK = MLA_NOPE + MLA_ROPE
MLA_Q_RANK, MLA_KV_RANK = 256, 128
MLA_SCALE = MLA_QK ** -0.5

ML_HEADS, ML_DK, ML_DV = 4, 64, 128
ML_CHUNK = 256

SW_HEADS, SW_KV_HEADS, SW_HD = 8, 2, 64
SW_REP = SW_HEADS // SW_KV_HEADS
SW_BLOCK = 128
SW_SCALE = SW_HD ** -0.5

RW_HEADS, RW_N = 8, 64
RW_DIM = RW_HEADS * RW_N
RW_W_RANK, RW_A_RANK, RW_G_RANK = 64, 64, 128
RW_DECAY_SCALE = 0.6065306597126334
RW_GN_EPS = 64e-5
RW_CHUNK = 64

_IN_WIDTHS = (
    MLA_Q_RANK, MLA_KV_RANK, MLA_ROPE,
    ML_HEADS * ML_DK, ML_HEADS * ML_DK, ML_HEADS * ML_DV,
    2 * ML_HEADS, 2 * ML_HEADS, ML_HEADS * ML_DV,
    SW_HEADS * SW_HD, SW_KV_HEADS * SW_HD, SW_KV_HEADS * SW_HD,
    RW_DIM, RW_DIM, RW_DIM, 2 * RW_W_RANK, 2 * RW_A_RANK, RW_G_RANK,
    N_BRANCH * D_MODEL,
)
_IN_OFFS = tuple(int(v) for v in np.concatenate([[0], np.cumsum(_IN_WIDTHS)]))

WA = 512
WB = 4 * 512 + 2 * LANES
WC = 1024 + 256 + 256
WD = 3 * 1024 + 256 + 256 + 128 + 128
RW_W1_BLK = 3 * 1024 // LANES
RW_A1_BLK = RW_W1_BLK + 2

LOG2E = 1.4426950408889634
V_HD = 64


def _cparams(sem):
    return pltpu.CompilerParams(dimension_semantics=sem, vmem_limit_bytes=VMEM_LIMIT_BYTES)


def _dot(a, b):
    return jnp.dot(a, b, preferred_element_type=F32)


def _dot_nt(a, b):
    return lax.dot_general(a, b, (((1,), (1,)), ((), ())), preferred_element_type=F32)


def _dot_tn(a, b):
    return lax.dot_general(a, b, (((0,), (0,)), ((), ())), preferred_element_type=F32)


def _split3(x):
    h = x.astype(BF16)
    r = x - h.astype(F32)
    m = r.astype(BF16)
    l = (r - m.astype(F32)).astype(BF16)
    return h, m, l


def _mask_dot(mask_bf16, x):
    h, m, l = _split3(x)
    return _dot(mask_bf16, h) + _dot(mask_bf16, m) + _dot(mask_bf16, l)


def _dot_mask(x, mask_bf16):
    h, m, l = _split3(x)
    return _dot(h, mask_bf16) + _dot(m, mask_bf16) + _dot(l, mask_bf16)


def _sigmoid(x):
    return 1.0 / (1.0 + jnp.exp(-x))


def _log_sigmoid(x):
    return jnp.minimum(x, 0.0) - jnp.log1p(jnp.exp(-jnp.abs(x)))


def _mm_kernel(x_ref, w_ref, b_ref, o_ref, *, pre, act):
    x = x_ref[...]
    if pre == "silu":
        xf = x.astype(F32)
        x = (xf * _sigmoid(xf)).astype(BF16)
    acc = _dot(x, w_ref[...]) + b_ref[...]
    if act == "sigmoid":
        acc = _sigmoid(acc)
    o_ref[...] = acc.astype(o_ref.dtype)


def _mm(x, w, bias=None, *, tm, tn, pre=None, act=None, out_dtype=F32, name="mm"):
    M, K = x.shape
    N = w.shape[1]
    assert M % tm == 0 and N % tn == 0, (M, tm, N, tn)
    if bias is None:
        bias = jnp.zeros((1, N), F32)
    return pl.pallas_call(
        functools.partial(_mm_kernel, pre=pre, act=act),
        grid=(M // tm, N // tn),
        in_specs=[pl.BlockSpec((tm, K), lambda i, j: (i, 0)),
                  pl.BlockSpec((K, tn), lambda i, j: (0, j)),
                  pl.BlockSpec((1, tn), lambda i, j: (0, j))],
        out_specs=pl.BlockSpec((tm, tn), lambda i, j: (i, j)),
        out_shape=jax.ShapeDtypeStruct((M, N), out_dtype),
        compiler_params=_cparams(("parallel", "parallel")),
        name=name,
    )(x, w, bias)


def _norm_mod_kernel(x_ref, g_ref, shift_ref, scale_ref, o_ref):
    x = x_ref[...]
    y = x * lax.rsqrt(jnp.mean(x * x, axis=-1, keepdims=True) + NORM_EPS)
    o_ref[...] = ((y * g_ref[...]) * (1.0 + scale_ref[...]) + shift_ref[...]).astype(o_ref.dtype)


def _norm_mod(x, g, mod, shift_idx, scale_idx, *, ts):
    Bm, Sm, D = x.shape
    return pl.pallas_call(
        _norm_mod_kernel,
        grid=(Bm, Sm // ts),
        in_specs=[pl.BlockSpec((None, ts, D), lambda b, s: (b, s, 0)),
                  pl.BlockSpec((1, D), lambda b, s: (0, 0)),
                  pl.BlockSpec((None, 1, D), lambda b, s: (b, 0, shift_idx)),
                  pl.BlockSpec((None, 1, D), lambda b, s: (b, 0, scale_idx))],
        out_specs=pl.BlockSpec((None, ts, D), lambda b, s: (b, s, 0)),
        out_shape=jax.ShapeDtypeStruct((Bm, Sm, D), BF16),
        compiler_params=_cparams(("parallel", "parallel")),
        name="norm_mod",
    )(x, g, mod, mod)


def _mlp_kernel(h_ref, x_ref, gate_ref, w1_ref, w2_ref, o_ref, acc_ref):
    j = pl.program_id(2)
    a = jnp.maximum(_dot(h_ref[...], w1_ref[...]), 0.0)
    part = _dot((a * a).astype(BF16), w2_ref[...])

    @pl.when(j == 0)
    def _():
        acc_ref[...] = part

    @pl.when(j > 0)
    def _():
        acc_ref[...] += part

    @pl.when(j == pl.num_programs(2) - 1)
    def _():
        o_ref[...] = x_ref[...] + gate_ref[...] * acc_ref[...]


def _mlp(h, x, mod, w1, w2, *, ts, tf):
    Bm, Sm, D = x.shape
    FF = w1.shape[1]
    return pl.pallas_call(
        _mlp_kernel,
        grid=(Bm, Sm // ts, FF // tf),
        in_specs=[pl.BlockSpec((None, ts, D), lambda b, s, j: (b, s, 0)),
                  pl.BlockSpec((None, ts, D), lambda b, s, j: (b, s, 0)),
                  pl.BlockSpec((None, 1, D), lambda b, s, j: (b, 0, 5)),
                  pl.BlockSpec((D, tf), lambda b, s, j: (0, j)),
                  pl.BlockSpec((tf, D), lambda b, s, j: (j, 0))],
        out_specs=pl.BlockSpec((None, ts, D), lambda b, s, j: (b, s, 0)),
        out_shape=jax.ShapeDtypeStruct((Bm, Sm, D), F32),
        scratch_shapes=[pltpu.VMEM((ts, D), F32)],
        compiler_params=_cparams(("parallel", "parallel", "arbitrary")),
        name="mlp",
    )(h, x, mod, w1, w2)


def _merge_kernel(h_ref, ya_ref, yb_ref, yc_ref, yd_ref, x_ref, gate_ref, wg_ref, wo_ref, wout_ref,
                  o_ref, acc_ref):
    i = pl.program_id(2)
    g = _sigmoid(_dot(h_ref[...], wg_ref[...]))
    for idx, y_ref in enumerate((ya_ref, yb_ref, yc_ref, yd_ref)):
        @pl.when(i == idx)
        def _(idx=idx, y_ref=y_ref):
            contrib = g * _dot(y_ref[...], wo_ref[...])
            if idx == 0:
                acc_ref[...] = contrib
            else:
                acc_ref[...] += contrib

    @pl.when(i == N_BRANCH - 1)
    def _():
        o_ref[...] = x_ref[...] + gate_ref[...] * _dot(acc_ref[...].astype(BF16), wout_ref[...])


def _merge(h, ya, yb, yc, yd, x, mod, wg, wo, wout, *, ts):
    Bm, Sm, D = x.shape

    def tok(width):
        return pl.BlockSpec((None, ts, width), lambda b, s, i: (b, s, 0))

    return pl.pallas_call(
        _merge_kernel,
        grid=(Bm, Sm // ts, N_BRANCH),
        in_specs=[tok(D), tok(ya.shape[-1]), tok(yb.shape[-1]), tok(yc.shape[-1]), tok(yd.shape[-1]), tok(D),
                  pl.BlockSpec((None, 1, D), lambda b, s, i: (b, 0, 2)),
                  pl.BlockSpec((None, D, D), lambda b, s, i: (i, 0, 0)),
                  pl.BlockSpec((None, wo.shape[1], D), lambda b, s, i: (i, 0, 0)),
                  pl.BlockSpec((D, D), lambda b, s, i: (0, 0))],
        out_specs=tok(D),
        out_shape=jax.ShapeDtypeStruct((Bm, Sm, D), F32),
        scratch_shapes=[pltpu.VMEM((ts, D), F32)],
        compiler_params=_cparams(("parallel", "parallel", "arbitrary")),
        name="merge",
    )(h, ya, yb, yc, yd, x, mod, wg, wo, wout)


def _rope_tables(n_tokens, start, rot):
    q = rot // 4
    t = jnp.arange(n_tokens, dtype=jnp.int32)
    pos_r = (t // GRID_W).astype(F32)
    pos_c = (t % GRID_W).astype(F32)
    inv = ROPE_BASE ** (-jnp.arange(q, dtype=F32) / q)
    ar = pos_r[:, None] * inv
    ac = pos_c[:, None] * inv
    zeros = jnp.zeros_like(ar)
    cos_t = jnp.concatenate([jnp.cos(ar), jnp.cos(ar), jnp.cos(ac), jnp.cos(ac)], -1)
    sin_prev = jnp.concatenate([zeros, jnp.sin(ar), zeros, jnp.sin(ac)], -1)
    sin_next = jnp.concatenate([-jnp.sin(ar), zeros, -jnp.sin(ac), zeros], -1)

    def place(tab, fill):
        left = jnp.full((n_tokens, start), fill, F32)
        right = jnp.full((n_tokens, LANES - start - rot), fill, F32)
        return jnp.concatenate([left, tab, right], -1)

    return place(cos_t, 1.0), place(sin_prev, 0.0), place(sin_next, 0.0)


def _rope(x, cos_t, sin_prev, sin_next, q):
    return (x * cos_t + pltpu.roll(x, q, 1) * sin_prev + pltpu.roll(x, LANES - q, 1) * sin_next)


def _mla_q_kernel(u_ref, qan_ref, wq_ref, qn_ref, cos_ref, sp_ref, sn_ref, q_ref, *, rope):
    qa = u_ref[...]
    qlat = (qa * lax.rsqrt(jnp.mean(qa * qa, axis=-1, keepdims=True) + NORM_EPS) * qan_ref[...]).astype(BF16)
    for h in range(MLA_HEADS):
        q = _dot(qlat, wq_ref[h])
        q = q * lax.rsqrt(jnp.sum(q * q, axis=-1, keepdims=True) * (1.0 / MLA_QK) + NORM_EPS) * qn_ref[...]
        if rope:
            q = _rope(q, cos_ref[...], sp_ref[...], sn_ref[...], MLA_ROPE // 4)
        q_ref[h] = (q * (MLA_SCALE * LOG2E)).astype(BF16)


def _mla_q(uA, qan, wq, qn, tabs, *, ts, rope):
    B, S, _ = uA.shape
    tab_spec = pl.BlockSpec((ts, LANES), lambda b, s: (s, 0))
    return pl.pallas_call(
        functools.partial(_mla_q_kernel, rope=rope),
        grid=(B, S // ts),
        in_specs=[pl.BlockSpec((None, ts, MLA_Q_RANK), lambda b, s: (b, s, 0)),
                  pl.BlockSpec((1, MLA_Q_RANK), lambda b, s: (0, 0)),
                  pl.BlockSpec((MLA_HEADS, MLA_Q_RANK, LANES), lambda b, s: (0, 0, 0)),
                  pl.BlockSpec((1, LANES), lambda b, s: (0, 0)),
                  tab_spec, tab_spec, tab_spec],
        out_specs=pl.BlockSpec((None, MLA_HEADS, ts, LANES), lambda b, s: (b, 0, s, 0)),
        out_shape=jax.ShapeDtypeStruct((B, MLA_HEADS, S, LANES), BF16),
        compiler_params=_cparams(("parallel", "parallel")),
        name="mla_q",
    )(uA, qan, wq, qn, *tabs)


def _with_ones(v):
    lane = lax.broadcasted_iota(jnp.int32, v.shape, 1)
    return jnp.where(lane == V_HD, 1.0, v)


def _mla_kv_kernel(kva_ref, kr_ref, kvn_ref, wk_ref, wv_ref, kn_ref, cos_ref, sp_ref, sn_ref,
                   k_ref, v_ref, ckv_ref, *, rope, normalise):
    c = kva_ref[...]
    if normalise:
        c = c * lax.rsqrt(jnp.mean(c * c, axis=-1, keepdims=True) + NORM_EPS) * kvn_ref[...]
    ckv_ref[...] = c
    cb = c.astype(BF16)
    for h in range(MLA_HEADS):
        k = _dot(cb, wk_ref[h]) + kr_ref[...]
        k = k * lax.rsqrt(jnp.sum(k * k, axis=-1, keepdims=True) * (1.0 / MLA_QK) + NORM_EPS) * kn_ref[...]
        if rope:
            k = _rope(k, cos_ref[...], sp_ref[...], sn_ref[...], MLA_ROPE // 4)
        k_ref[h] = k.astype(BF16)
        v_ref[h] = _with_ones(_dot(cb, wv_ref[h])).astype(BF16)


def _mla_kv(kva_src, kva_blk, kr_src, kr_blk, kvn, wk, wv, kn, tabs, *, ts, rope, normalise):
    B, S, _ = kva_src.shape
    tab_spec = pl.BlockSpec((ts, LANES), lambda b, s: (s, 0))
    head_out = pl.BlockSpec((None, MLA_HEADS, ts, LANES), lambda b, s: (b, 0, s, 0))
    w_spec = pl.BlockSpec((MLA_HEADS, LANES, LANES), lambda b, s: (0, 0, 0))
    return pl.pallas_call(
        functools.partial(_mla_kv_kernel, rope=rope, normalise=normalise),
        grid=(B, S // ts),
        in_specs=[pl.BlockSpec((None, ts, LANES), lambda b, s: (b, s, kva_blk)),
                  pl.BlockSpec((None, ts, LANES), lambda b, s: (b, s, kr_blk)),
                  pl.BlockSpec((1, LANES), lambda b, s: (0, 0)),
                  w_spec, w_spec,
                  pl.BlockSpec((1, LANES), lambda b, s: (0, 0)),
                  tab_spec, tab_spec, tab_spec],
        out_specs=[head_out, head_out, pl.BlockSpec((None, ts, LANES), lambda b, s: (b, s, 0))],
        out_shape=[jax.ShapeDtypeStruct((B, MLA_HEADS, S, LANES), BF16),
                   jax.ShapeDtypeStruct((B, MLA_HEADS, S, LANES), BF16),
                   jax.ShapeDtypeStruct((B, S, LANES), F32)],
        compiler_params=_cparams(("parallel", "parallel")),
        name="mla_kv",
    )(kva_src, kr_src, kvn, wk, wv, kn, *tabs)


def _pair_heads(lo, hi):
    lane = lax.broadcasted_iota(jnp.int32, (1, LANES), 1)
    return jnp.where(lane < V_HD, lo, pltpu.roll(hi, V_HD, 1))


def _attn_kernel(q_ref, k_ref, v_ref, sink_ref, o_ref, *, use_sink, shared_kv):
    heads = range(2)
    kv = [0, 0] if shared_kv else [0, 1]
    s = [_dot_nt(q_ref[j], k_ref[kv[j]]) for j in heads]
    m = [jnp.max(s[j], axis=-1, keepdims=True) for j in heads]
    if use_sink:
        snk = [sink_ref[j][:, 0:1] * LOG2E for j in heads]
        m = [jnp.maximum(m[j], snk[j]) for j in heads]
    o = [_dot(jnp.exp2(s[j] - m[j]).astype(BF16), v_ref[kv[j]]) for j in heads]
    outs = []
    for j in heads:
        l = o[j][:, V_HD:V_HD + 1]
        if use_sink:
            l = l + jnp.exp2(snk[j] - m[j])
        outs.append(o[j] / l)
    o_ref[...] = _pair_heads(outs[0], outs[1]).astype(o_ref.dtype)


def _attn(q, k, v, sink, *, tq, use_sink):
    B, H, Sq, _ = q.shape
    G, Sk = k.shape[1], k.shape[2]
    rep = H // G
    assert H % 2 == 0 and (rep == 1 or rep % 2 == 0)
    nkv = 2 if rep == 1 else 1
    kv_spec = pl.BlockSpec((None, nkv, Sk, LANES), lambda b, hp, i: (b, (2 * hp) // (rep * nkv), 0, 0))
    return pl.pallas_call(
        functools.partial(_attn_kernel, use_sink=use_sink, shared_kv=rep > 1),
        grid=(B, H // 2, Sq // tq),
        in_specs=[pl.BlockSpec((None, 2, tq, LANES), lambda b, hp, i: (b, hp, i, 0)),
                  kv_spec, kv_spec,
                  pl.BlockSpec((2, 1, LANES), lambda b, hp, i: (hp, 0, 0))],
        out_specs=pl.BlockSpec((None, tq, LANES), lambda b, hp, i: (b, i, hp)),
        out_shape=jax.ShapeDtypeStruct((B, Sq, H * V_HD), BF16),
        compiler_params=_cparams(("parallel", "parallel", "parallel")),
        name="attn",
    )(q, k, v, sink)


def _win_attn_kernel(q_ref, kp_ref, kc_ref, kn_ref, vp_ref, vc_ref, vn_ref, kx_ref, vx_ref, sink_ref, o_ref):
    i = pl.program_id(2)
    nb = pl.num_programs(2)
    blk = SW_BLOCK
    kk = jnp.concatenate([kp_ref[...], kc_ref[...], kn_ref[...], kx_ref[...]], axis=0)
    vv = jnp.concatenate([vp_ref[...], vc_ref[...], vn_ref[...], vx_ref[...]], axis=0)
    nk = kk.shape[0]
    qi = lax.broadcasted_iota(jnp.int32, (blk, nk), 0)
    kj = lax.broadcasted_iota(jnp.int32, (blk, nk), 1)
    first = (i == 0).astype(jnp.int32)
    last = (i == nb - 1).astype(jnp.int32)
    lo = qi * (1 - first) + blk * first
    hi = (qi + 2 * blk) * (1 - last) + (2 * blk - 1) * last
    mask = (kj >= lo) & ((kj <= hi) | (kj >= 3 * blk))
    heads = range(SW_REP)
    s = [jnp.where(mask, _dot_nt(q_ref[r], kk), NEG_BIG) for r in heads]
    snk = [sink_ref[r][:, 0:1] * LOG2E for r in heads]
    m = [jnp.maximum(jnp.max(s[r], axis=-1, keepdims=True), snk[r]) for r in heads]
    o = [_dot(jnp.exp2(s[r] - m[r]).astype(BF16), vv) for r in heads]
    outs = [o[r] / (o[r][:, V_HD:V_HD + 1] + jnp.exp2(snk[r] - m[r])) for r in heads]
    for pr in range(SW_REP // 2):
        o_ref[:, pr * LANES:(pr + 1) * LANES] = _pair_heads(outs[2 * pr], outs[2 * pr + 1]).astype(o_ref.dtype)


def _win_attn(q, k, v, kx, vx, sink):
    B, H, S, _ = q.shape
    G = k.shape[1]
    C = kx.shape[2]
    nb = S // SW_BLOCK
    blk = SW_BLOCK

    def kv_spec(off):
        return pl.BlockSpec((None, None, blk, LANES),
                            lambda b, g, i: (b, g, jnp.clip(i + off, 0, nb - 1), 0))

    ctx_spec = pl.BlockSpec((None, None, C, LANES), lambda b, g, i: (b, g, 0, 0))
    return pl.pallas_call(
        _win_attn_kernel,
        grid=(B, G, nb),
        in_specs=[pl.BlockSpec((None, SW_REP, blk, LANES), lambda b, g, i: (b, g, i, 0)),
                  kv_spec(-1), kv_spec(0), kv_spec(1), kv_spec(-1), kv_spec(0), kv_spec(1),
                  ctx_spec, ctx_spec,
                  pl.BlockSpec((None, SW_REP, 1, LANES), lambda b, g, i: (g, 0, 0, 0))],
        out_specs=pl.BlockSpec((None, blk, SW_REP * V_HD), lambda b, g, i: (b, i, g)),
        out_shape=jax.ShapeDtypeStruct((B, S, H * V_HD), BF16),
        compiler_params=_cparams(("parallel", "parallel", "parallel")),
        name="win_attn",
    )(q, k, k, k, v, v, v, kx, vx, sink)


def _swa_prep_kernel(u_ref, qn_ref, kn_ref, cos_ref, sp_ref, sn_ref, q_ref, k_ref, v_ref, kraw_ref, *, rope):
    def norm(x, g):
        return x * lax.rsqrt(jnp.sum(x * x, axis=-1, keepdims=True) * (1.0 / SW_HD) + NORM_EPS) * g

    def rot(x):
        if not rope:
            return x
        return _rope(x, cos_ref[...], sp_ref[...], sn_ref[...], SW_HD // 4)

    for h in range(SW_HEADS):
        x = norm(u_ref[:, h * LANES:(h + 1) * LANES], qn_ref[...])
        q_ref[h] = (rot(x) * (SW_SCALE * LOG2E)).astype(BF16)
    k0 = SW_HEADS * LANES
    v0 = k0 + SW_KV_HEADS * LANES
    for g in range(SW_KV_HEADS):
        x = norm(u_ref[:, k0 + g * LANES:k0 + (g + 1) * LANES], kn_ref[...])
        kraw_ref[:, g * LANES:(g + 1) * LANES] = x
        k_ref[g] = rot(x).astype(BF16)
        v_ref[g] = _with_ones(u_ref[:, v0 + g * LANES:v0 + (g + 1) * LANES]).astype(BF16)


def _swa_prep(uC, qn, kn, tabs, *, ts, rope):
    B, S, _ = uC.shape
    tab_spec = pl.BlockSpec((ts, LANES), lambda b, s: (s, 0))
    v_spec = pl.BlockSpec((None, SW_KV_HEADS, ts, LANES), lambda b, s: (b, 0, s, 0))
    v_shape = jax.ShapeDtypeStruct((B, SW_KV_HEADS, S, LANES), BF16)
    return pl.pallas_call(
        functools.partial(_swa_prep_kernel, rope=rope),
        grid=(B, S // ts),
        in_specs=[pl.BlockSpec((None, ts, WC), lambda b, s: (b, s, 0)),
                  pl.BlockSpec((1, LANES), lambda b, s: (0, 0)),
                  pl.BlockSpec((1, LANES), lambda b, s: (0, 0)),
                  tab_spec, tab_spec, tab_spec],
        out_specs=[pl.BlockSpec((None, SW_HEADS, ts, LANES), lambda b, s: (b, 0, s, 0)),
                   pl.BlockSpec((None, SW_KV_HEADS, ts, LANES), lambda b, s: (b, 0, s, 0)),
                   v_spec,
                   pl.BlockSpec((None, ts, SW_KV_HEADS * LANES), lambda b, s: (b, s, 0))],
        out_shape=[jax.ShapeDtypeStruct((B, SW_HEADS, S, LANES), BF16),
                   jax.ShapeDtypeStruct((B, SW_KV_HEADS, S, LANES), BF16),
                   v_shape,
                   jax.ShapeDtypeStruct((B, S, SW_KV_HEADS * LANES), F32)],
        compiler_params=_cparams(("parallel", "parallel")),
        name="swa_prep",
    )(uC, qn, kn, *tabs)


def _dir_masks(L, d):
    row = lax.broadcasted_iota(jnp.int32, (L, L), 0)
    col = lax.broadcasted_iota(jnp.int32, (L, L), 1)
    if d == 0:
        return col <= row, col < row
    return col >= row, col > row


def _mlstm_kernel(qkvf_ref, qkvb_ref, gcf_ref, gcb_ref, grf_ref, grb_ref, bc_ref, br_ref, C0_ref, n0_ref, m0_ref,
                  hf_ref, hb_ref, Cout_ref, nout_ref, mout_ref, C_s, n_s, m_s):
    c = pl.program_id(1)
    L = ML_CHUNK
    H = ML_HEADS

    @pl.when(c == 0)
    def _():
        C_s[...] = C0_ref[...]
        n_s[...] = n0_ref[...]
        m_s[...] = m0_ref[...]

    chains = []
    for d, (qkv_ref, gc_ref, gr_ref, h_ref) in enumerate(((qkvf_ref, gcf_ref, grf_ref, hf_ref),
                                                          (qkvb_ref, gcb_ref, grb_ref, hb_ref))):
        incl, _ = _dir_masks(L, d)
        incl_t, _ = _dir_masks(L, 1 - d)
        incl_b = jnp.where(incl, 1.0, 0.0).astype(BF16)
        incl_t_b = jnp.where(incl_t, 1.0, 0.0).astype(BF16)
        gcol = gc_ref[...] + bc_ref[d]
        grow = gr_ref[...] + br_ref[d]
        lane = lax.broadcasted_iota(jnp.int32, gcol.shape, 1)
        gcol = jnp.where(lane < H, gcol, _log_sigmoid(gcol))
        rowi = lax.broadcasted_iota(jnp.int32, grow.shape, 0)
        grow = jnp.where(rowi < H, grow, _log_sigmoid(grow))
        bcol = _mask_dot(incl_b, gcol)
        brow = _dot_mask(grow, incl_t_b)
        tot = jnp.sum(gcol, axis=0, keepdims=True)
        for hh in range(H):
            ch = dict(d=d, hh=hh, h_ref=h_ref, incl=incl)
            ch["q"] = qkv_ref[:, hh * LANES:(hh + 1) * LANES] * (ML_DK ** -0.5)
            ch["k"] = qkv_ref[:, (H + hh) * LANES:(H + hh + 1) * LANES]
            ch["vb"] = qkv_ref[:, (2 * H + hh) * LANES:(2 * H + hh + 1) * LANES].astype(BF16)
            ch["qb"] = ch["q"].astype(BF16)
            ch["kb"] = ch["k"].astype(BF16)
            ch["i_col"] = gcol[:, hh:hh + 1]
            ch["b_col"] = bcol[:, H + hh:H + hh + 1]
            ch["i_row"] = grow[hh:hh + 1, :]
            ch["b_row"] = brow[H + hh:H + hh + 1, :]
            ch["bL"] = tot[:, H + hh:H + hh + 1]
            ch["C"] = C_s[d, hh]
            ch["n"] = n_s[d, hh]
            ch["m"] = m_s[d, hh][:, 0:1]
            chains.append(ch)

    for ch in chains:
        ch["S"] = _dot_nt(ch["qb"], ch["kb"])
        ch["qC"] = _dot(ch["qb"], ch["C"].astype(BF16))
    for ch in chains:
        D = jnp.where(ch["incl"], ch["b_col"] - ch["b_row"] + ch["i_row"], -jnp.inf)
        inter = ch["b_col"] + ch["m"]
        m_t = jnp.maximum(inter, jnp.max(D, axis=-1, keepdims=True))
        ch["W"] = ch["S"] * jnp.exp(D - m_t)
        ch["a_in"] = jnp.exp(inter - m_t)
        ch["m_t"] = m_t
        g_row = ch["bL"] - ch["b_row"] + ch["i_row"]
        m_new = jnp.maximum(ch["bL"] + ch["m"], jnp.max(g_row, axis=-1, keepdims=True))
        wk = jnp.exp(ch["bL"] - ch["b_col"] + ch["i_col"] - m_new)
        ch["decay"] = jnp.exp(ch["bL"] + ch["m"] - m_new)
        ch["m_new"] = m_new
        ch["ks"] = ch["k"] * wk
    for ch in chains:
        ch["Wv"] = _dot(ch["W"].astype(BF16), ch["vb"])
        ch["kv"] = _dot_tn(ch["ks"].astype(BF16), ch["vb"])
    for ch in chains:
        d, hh = ch["d"], ch["hh"]
        num = ch["Wv"] + ch["a_in"] * ch["qC"]
        den = (jnp.sum(ch["W"], axis=-1, keepdims=True)
               + ch["a_in"] * jnp.sum(ch["q"] * ch["n"], axis=-1, keepdims=True))
        ch["h_ref"][:, hh * LANES:(hh + 1) * LANES] = num / jnp.maximum(jnp.abs(den), jnp.exp(-ch["m_t"]))
        C_s[d, hh] = ch["decay"] * ch["C"] + ch["kv"]
        n_s[d, hh] = ch["decay"] * ch["n"] + jnp.sum(ch["ks"], axis=0, keepdims=True)
        m_s[d, hh] = jnp.broadcast_to(ch["m_new"], (1, LANES))

    @pl.when(c == pl.num_programs(1) - 1)
    def _():
        Cout_ref[...] = C_s[...]
        nout_ref[...] = n_s[...]
        mout_ref[...] = m_s[...]


def _mlstm(uB, g_rows, bias_col, bias_row, C0, n0, m0):
    B, S, _ = uB.shape
    L, H = ML_CHUNK, ML_HEADS
    nc = S // L
    gate_blk = 4 * 512 // LANES
    rev = lambda c: nc - 1 - c

    st_spec = lambda shape: pl.BlockSpec((None, 2) + shape, lambda b, c: (b, 0) + (0,) * len(shape))
    return pl.pallas_call(
        _mlstm_kernel,
        grid=(B, nc),
        in_specs=[pl.BlockSpec((None, L, 3 * 512), lambda b, c: (b, c, 0)),
                  pl.BlockSpec((None, L, 3 * 512), lambda b, c: (b, rev(c), 0)),
                  pl.BlockSpec((None, L, LANES), lambda b, c: (b, c, gate_blk)),
                  pl.BlockSpec((None, L, LANES), lambda b, c: (b, rev(c), gate_blk + 1)),
                  pl.BlockSpec((None, None, None, 2 * H, L), lambda b, c: (b, 0, c, 0, 0)),
                  pl.BlockSpec((None, None, None, 2 * H, L), lambda b, c: (b, 1, rev(c), 0, 0)),
                  pl.BlockSpec((2, 1, LANES), lambda b, c: (0, 0, 0)),
                  pl.BlockSpec((2, 2 * H, L), lambda b, c: (0, 0, 0)),
                  st_spec((H, LANES, LANES)), st_spec((H, 1, LANES)), st_spec((H, 1, LANES))],
        out_specs=[pl.BlockSpec((None, L, H * LANES), lambda b, c: (b, c, 0)),
                   pl.BlockSpec((None, L, H * LANES), lambda b, c: (b, rev(c), 0)),
                   st_spec((H, LANES, LANES)), st_spec((H, 1, LANES)), st_spec((H, 1, LANES))],
        out_shape=[jax.ShapeDtypeStruct((B, S, H * LANES), F32),
                   jax.ShapeDtypeStruct((B, S, H * LANES), F32),
                   jax.ShapeDtypeStruct((B, 2, H, LANES, LANES), F32),
                   jax.ShapeDtypeStruct((B, 2, H, 1, LANES), F32),
                   jax.ShapeDtypeStruct((B, 2, H, 1, LANES), F32)],
        scratch_shapes=[pltpu.VMEM((2, H, LANES, LANES), F32),
                        pltpu.VMEM((2, H, 1, LANES), F32),
                        pltpu.VMEM((2, H, 1, LANES), F32)],
        compiler_params=_cparams(("parallel", "arbitrary")),
        name="mlstm",
    )(uB, uB, uB, uB, g_rows, g_rows, bias_col, bias_row, C0, n0, m0)


def _mlstm_post_kernel(hf_ref, hb_ref, o_ref, g_ref, y_ref):
    for hh in range(ML_HEADS):
        sl = slice(hh * LANES, (hh + 1) * LANES)
        x = hf_ref[:, sl] + hb_ref[:, sl]
        y = x * lax.rsqrt(jnp.mean(x * x, axis=-1, keepdims=True) + NORM_EPS) * g_ref[...]
        y_ref[:, sl] = (y * _sigmoid(o_ref[:, sl])).astype(y_ref.dtype)


def _mlstm_post(h_f, h_b, uB, g, *, ts):
    B, S, W = h_f.shape
    return pl.pallas_call(
        _mlstm_post_kernel,
        grid=(B, S // ts),
        in_specs=[pl.BlockSpec((None, ts, W), lambda b, s: (b, s, 0)),
                  pl.BlockSpec((None, ts, W), lambda b, s: (b, s, 0)),
                  pl.BlockSpec((None, ts, W), lambda b, s: (b, s, 3)),
                  pl.BlockSpec((1, LANES), lambda b, s: (0, 0))],
        out_specs=pl.BlockSpec((None, ts, W), lambda b, s: (b, s, 0)),
        out_shape=jax.ShapeDtypeStruct((B, S, W), BF16),
        compiler_params=_cparams(("parallel", "parallel")),
        name="mlstm_post",
    )(h_f, h_b, uB, g)


def _rwkv_kernel(rkvf_ref, rkvb_ref, w1f_ref, w1b_ref, a1f_ref, a1b_ref, w2_ref, w0_ref, a2_ref, a0_ref,
                 kk_ref, ka_ref, S0_ref, yf_ref, yb_ref, Sout_ref, S_s):
    c = pl.program_id(1)
    T = RW_CHUNK
    HW = RW_HEADS * LANES

    @pl.when(c == 0)
    def _():
        S_s[...] = S0_ref[...]

    eye = (lax.broadcasted_iota(jnp.int32, (T, T), 0) == lax.broadcasted_iota(jnp.int32, (T, T), 1)).astype(F32)

    chains = []
    for d, (rkv_ref, w1_ref, a1_ref, y_ref) in enumerate(((rkvf_ref, w1f_ref, a1f_ref, yf_ref),
                                                          (rkvb_ref, w1b_ref, a1b_ref, yb_ref))):
        incl, _ = _dir_masks(T, d)
        incl_b = jnp.where(incl, 1.0, 0.0).astype(BF16)
        row2 = lax.broadcasted_iota(jnp.int32, (2 * T, T), 0)
        col2 = lax.broadcasted_iota(jnp.int32, (2 * T, T), 1)
        low = (row2 >= T).astype(jnp.int32)
        t2 = row2 - T * low
        stacked = (col2 < t2 + low) if d == 0 else (col2 > t2 - low)
        r = rkv_ref[:, 0:HW]
        k = rkv_ref[:, HW:2 * HW]
        v = rkv_ref[:, 2 * HW:3 * HW]
        w_pre = _dot(jnp.tanh(w1_ref[...]).astype(BF16), w2_ref[d]) + w0_ref[d]
        logw = -RW_DECAY_SCALE * _sigmoid(w_pre)
        a = _sigmoid(_dot(a1_ref[...].astype(BF16), a2_ref[d]) + a0_ref[d])
        kappa = k * kk_ref[...]
        kt = k * (1.0 + (a - 1.0) * ka_ref[...])
        cs = _mask_dot(incl_b, logw)
        tot = jnp.sum(logw, axis=0, keepdims=True)
        e_pos = jnp.exp(cs)
        e_neg = jnp.exp(-cs)
        e_prev = jnp.exp(cs - logw)
        e_tail = jnp.exp(tot - cs)
        g_tot = jnp.exp(tot)
        for hh in range(RW_HEADS):
            sl = slice(hh * LANES, (hh + 1) * LANES)
            kap = kappa[:, sl]
            kh = kap * lax.rsqrt(jnp.sum(kap * kap, axis=-1, keepdims=True) + 1e-12)
            bb = a[:, sl] * kh
            kth = kt[:, sl]
            chains.append(dict(
                d=d, hh=hh, sl=sl, y_ref=y_ref, g_tot=g_tot[:, sl],
                mask=stacked,
                QR=jnp.concatenate([kh * e_prev[:, sl], r[:, sl] * e_pos[:, sl]], axis=0).astype(BF16),
                Kh=(kth * e_neg[:, sl]).astype(BF16), Bh=(bb * e_neg[:, sl]).astype(BF16),
                KB=jnp.concatenate([kth * e_tail[:, sl], bb * e_tail[:, sl]], axis=0).astype(BF16),
                V=v[:, sl].astype(BF16)))

    for ch in chains:
        sk = jnp.where(ch["mask"], _dot_nt(ch["QR"], ch["Kh"]), 0.0)
        sb = jnp.where(ch["mask"], _dot_nt(ch["QR"], ch["Bh"]), 0.0)
        ch["LkAk"] = sk.astype(BF16)
        ch["Ab"] = sb[T:, :].astype(BF16)
        ch["P"] = -sb[:T, :]
        ch["X"] = eye + ch["P"]
    for _ in range(5):
        for ch in chains:
            Pb = ch["P"].astype(BF16)
            ch["P"] = _dot(Pb, Pb)
        for ch in chains:
            ch["X"] = ch["X"] + _dot(ch["X"].astype(BF16), ch["P"].astype(BF16))
    for ch in chains:
        ch["S0"] = S_s[ch["d"], ch["hh"]]
        ch["T1"] = _dot_nt(ch["QR"], ch["S0"].astype(BF16)) + _dot(ch["LkAk"], ch["V"])
    for ch in chains:
        ch["Zb"] = _dot(ch["X"].astype(BF16), ch["T1"][:T, :].astype(BF16)).astype(BF16)
    for ch in chains:
        ch["y_ref"][:, ch["sl"]] = ch["T1"][T:, :] - _dot(ch["Ab"], ch["Zb"])
        vz = jnp.concatenate([ch["V"], -ch["Zb"]], axis=0)
        S_s[ch["d"], ch["hh"]] = ch["S0"] * ch["g_tot"] + _dot_tn(vz, ch["KB"])

    @pl.when(c == pl.num_programs(1) - 1)
    def _():
        Sout_ref[...] = S_s[...]


def _rwkv(uD, w2, w0, a2, a0, kk, ka, S0):
    B, S, _ = uD.shape
    T = RW_CHUNK
    nc = S // T
    HW = RW_HEADS * LANES
    rev = lambda c: nc - 1 - c
    full = lambda shape: pl.BlockSpec(shape, lambda b, c: (0,) * len(shape))
    st_spec = pl.BlockSpec((None, 2, RW_HEADS, LANES, LANES), lambda b, c: (b, 0, 0, 0, 0))
    return pl.pallas_call(
        _rwkv_kernel,
        grid=(B, nc),
        in_specs=[pl.BlockSpec((None, T, 3 * HW), lambda b, c: (b, c, 0)),
                  pl.BlockSpec((None, T, 3 * HW), lambda b, c: (b, rev(c), 0)),
                  pl.BlockSpec((None, T, LANES), lambda b, c: (b, c, RW_W1_BLK)),
                  pl.BlockSpec((None, T, LANES), lambda b, c: (b, rev(c), RW_W1_BLK + 1)),
                  pl.BlockSpec((None, T, LANES), lambda b, c: (b, c, RW_A1_BLK)),
                  pl.BlockSpec((None, T, LANES), lambda b, c: (b, rev(c), RW_A1_BLK + 1)),
                  full((2, LANES, HW)), full((2, 1, HW)), full((2, LANES, HW)), full((2, 1, HW)),
                  full((1, HW)), full((1, HW)),
                  st_spec],
        out_specs=[pl.BlockSpec((None, T, HW), lambda b, c: (b, c, 0)),
                   pl.BlockSpec((None, T, HW), lambda b, c: (b, rev(c), 0)),
                   st_spec],
        out_shape=[jax.ShapeDtypeStruct((B, S, HW), F32),
                   jax.ShapeDtypeStruct((B, S, HW), F32),
                   jax.ShapeDtypeStruct((B, 2, RW_HEADS, LANES, LANES), F32)],
        scratch_shapes=[pltpu.VMEM((2, RW_HEADS, LANES, LANES), F32)],
        compiler_params=_cparams(("parallel", "arbitrary")),
        name="rwkv",
    )(uD, uD, uD, uD, uD, uD, w2, w0, a2, a0, kk, ka, S0)


def _rwkv_post_kernel(yf_ref, yb_ref, u_ref, a2_ref, a0_ref, ka_ref, uu_ref, g2_ref, gng_ref, gnb_ref, o_ref):
    HW = RW_HEADS * LANES
    r = u_ref[:, 0:HW]
    k = u_ref[:, HW:2 * HW]
    v = u_ref[:, 2 * HW:3 * HW]
    a1_0 = RW_A1_BLK * LANES
    g1_0 = a1_0 + 2 * LANES
    g = _dot(_sigmoid(u_ref[:, g1_0:g1_0 + LANES]).astype(BF16), g2_ref[...])
    rk = []
    for dd in range(2):
        a1 = u_ref[:, a1_0 + dd * LANES:a1_0 + (dd + 1) * LANES]
        a = _sigmoid(_dot(a1.astype(BF16), a2_ref[dd]) + a0_ref[dd])
        kt = k * (1.0 + (a - 1.0) * ka_ref[...])
        rk.append(r * kt * uu_ref[dd])
    y = yf_ref[...] + yb_ref[...]
    lane = lax.broadcasted_iota(jnp.int32, (1, LANES), 1)
    real = lane < RW_N
    outs = []
    for hh in range(RW_HEADS):
        sl = slice(hh * LANES, (hh + 1) * LANES)
        yh = y[:, sl]
        mu = jnp.sum(yh, axis=-1, keepdims=True) * (1.0 / RW_N)
        dev = jnp.where(real, yh - mu, 0.0)
        var = jnp.sum(dev * dev, axis=-1, keepdims=True) * (1.0 / RW_N)
        yn = dev * lax.rsqrt(var + RW_GN_EPS) * gng_ref[:, sl] + gnb_ref[:, sl]
        bonus = (jnp.sum(rk[0][:, sl], axis=-1, keepdims=True)
                 + jnp.sum(rk[1][:, sl], axis=-1, keepdims=True)) * v[:, sl]
        outs.append((yn + bonus) * g[:, sl])
    for pr in range(RW_HEADS // 2):
        o_ref[:, pr * LANES:(pr + 1) * LANES] = _pair_heads(outs[2 * pr], outs[2 * pr + 1]).astype(o_ref.dtype)


def _rwkv_post(y_f, y_b, uD, a2, a0, ka, uu, g2, gng, gnb, *, ts):
    B, S, HW = y_f.shape
    full = lambda shape: pl.BlockSpec(shape, lambda b, s: (0,) * len(shape))
    return pl.pallas_call(
        _rwkv_post_kernel,
        grid=(B, S // ts),
        in_specs=[pl.BlockSpec((None, ts, HW), lambda b, s: (b, s, 0)),
                  pl.BlockSpec((None, ts, HW), lambda b, s: (b, s, 0)),
                  pl.BlockSpec((None, ts, WD), lambda b, s: (b, s, 0)),
                  full((2, LANES, HW)), full((2, 1, HW)), full((1, HW)), full((2, 1, HW)),
                  full((LANES, HW)), full((1, HW)), full((1, HW))],
        out_specs=pl.BlockSpec((None, ts, RW_DIM), lambda b, s: (b, s, 0)),
        out_shape=jax.ShapeDtypeStruct((B, S, RW_DIM), BF16),
        compiler_params=_cparams(("parallel", "parallel")),
        name="rwkv_post",
    )(y_f, y_b, uD, a2, a0, ka, uu, g2, gng, gnb)


def _pad_heads(w, nh, hd, axis=-1):
    axis = axis % w.ndim
    shp = w.shape
    w = w.reshape(shp[:axis] + (nh, hd) + shp[axis + 1:])
    pad = [(0, 0)] * w.ndim
    pad[axis + 1] = (0, LANES - hd)
    w = jnp.pad(w, pad)
    return w.reshape(shp[:axis] + (nh * LANES,) + shp[axis + 1:])


def _pad_cols(w, width):
    return jnp.pad(w, [(0, 0)] * (w.ndim - 1) + [(0, width - w.shape[-1])])


def _layer_weights(P):
    w_in = P["w_in"]
    col = lambda i: w_in[:, _IN_OFFS[i]:_IN_OFFS[i + 1]]
    zc = lambda n: jnp.zeros((D_MODEL, n), F32)
    W = {}
    W["wA"] = jnp.concatenate([col(0), col(1), zc(MLA_NOPE), col(2), zc(LANES - MLA_NOPE - MLA_ROPE)], 1).astype(BF16)
    gi, gf = col(6), col(7)
    gates = [jnp.concatenate([gi[:, d * ML_HEADS:(d + 1) * ML_HEADS], gf[:, d * ML_HEADS:(d + 1) * ML_HEADS],
                              zc(LANES - 2 * ML_HEADS)], 1) for d in range(2)]
    W["wB"] = jnp.concatenate([_pad_heads(col(3), ML_HEADS, ML_DK), _pad_heads(col(4), ML_HEADS, ML_DK),
                               col(5), col(8)] + gates, 1).astype(BF16)
    W["wC"] = jnp.concatenate([_pad_heads(col(9), SW_HEADS, SW_HD), _pad_heads(col(10), SW_KV_HEADS, SW_HD),
                               _pad_heads(col(11), SW_KV_HEADS, SW_HD)], 1).astype(BF16)
    W["wD"] = jnp.concatenate([_pad_heads(col(12), RW_HEADS, RW_N), _pad_heads(col(13), RW_HEADS, RW_N),
                               _pad_heads(col(14), RW_HEADS, RW_N),
                               _pad_heads(col(15), 2, RW_W_RANK), _pad_heads(col(16), 2, RW_A_RANK),
                               col(17), zc(LANES)], 1).astype(BF16)
    W["wG"] = jnp.transpose(col(18).reshape(D_MODEL, N_BRANCH, D_MODEL), (1, 0, 2)).astype(BF16)
    W["wO"] = jnp.stack([P["mla_w_o"], P["mlstm_w_o"], P["swa_w_o"], P["rwkv_w_o"]]).astype(BF16)
    W["w_out"] = P["w_out"].astype(BF16)
    W["mlp_w1"] = P["mlp_w1"].astype(BF16)
    W["mlp_w2"] = P["mlp_w2"].astype(BF16)
    W["norm1"] = P["norm1"][None, :]
    W["norm2"] = P["norm2"][None, :]
    W["q_a_norm"] = P["mla_q_a_norm"][None, :]
    W["kv_a_norm"] = P["mla_kv_a_norm"][None, :]
    wq = P["mla_w_uq"].reshape(MLA_Q_RANK, MLA_HEADS, MLA_QK)
    W["wq"] = jnp.transpose(_pad_cols(wq, LANES), (1, 0, 2)).astype(BF16)
    wkv = P["mla_w_ukv"].reshape(MLA_KV_RANK, MLA_HEADS, MLA_NOPE + MLA_V)
    W["wk"] = jnp.transpose(_pad_cols(wkv[..., :MLA_NOPE], LANES), (1, 0, 2)).astype(BF16)
    W["wv"] = jnp.transpose(_pad_cols(wkv[..., MLA_NOPE:], LANES), (1, 0, 2)).astype(BF16)
    W["q_norm"] = _pad_cols(P["mla_q_norm"][None, :], LANES)
    W["k_norm"] = _pad_cols(P["mla_k_norm"][None, :], LANES)
    ib, fb = P["mlstm_i_bias"], P["mlstm_f_bias"]
    bias = jnp.concatenate([ib, fb], -1)
    W["ml_bias_col"] = _pad_cols(bias, LANES)[:, None, :]
    W["ml_bias_row"] = jnp.broadcast_to(bias[:, :, None], (2, 2 * ML_HEADS, ML_CHUNK))
    W["ml_norm"] = P["mlstm_norm"][None, :]
    W["swa_q_norm"] = _pad_cols(P["swa_q_norm"][None, :], LANES)
    W["swa_k_norm"] = _pad_cols(P["swa_k_norm"][None, :], LANES)
    sink = jnp.broadcast_to(P["swa_sink"][:, None, None], (SW_HEADS, 1, LANES))
    W["sink"] = sink
    W["sink_g"] = sink.reshape(SW_KV_HEADS, SW_REP, 1, LANES)
    ph = lambda w: _pad_heads(w, RW_HEADS, RW_N)
    W["rw_w2"] = jnp.pad(ph(P["rwkv_w2"]), ((0, 0), (0, LANES - RW_W_RANK), (0, 0))).astype(BF16)
    W["rw_a2"] = jnp.pad(ph(P["rwkv_a2"]), ((0, 0), (0, LANES - RW_A_RANK), (0, 0))).astype(BF16)
    W["rw_w0"] = ph(P["rwkv_w0"])[:, None, :]
    W["rw_a0"] = ph(P["rwkv_a0"])[:, None, :]
    W["rw_g2"] = ph(P["rwkv_g2"]).astype(BF16)
    W["rw_kk"] = ph(P["rwkv_kk"])[None, :]
    W["rw_ka"] = ph(P["rwkv_ka"])[None, :]
    W["rw_u"] = ph(P["rwkv_u"])[:, None, :]
    W["rw_gn_g"] = ph(P["rwkv_gn_g"])[None, :]
    W["rw_gn_b"] = ph(P["rwkv_gn_b"])[None, :]
    return W


def _tile(n, pref):
    t = min(n, pref)
    assert n % t == 0, (n, t)
    return t


def _trunk_layer(x, mod, W, ctx, B, S):
    Bm, Sm, D = x.shape
    ntok = Bm * Sm
    ts = _tile(Sm, 1024)
    tm = _tile(ntok, 1024)
    tsq = _tile(S, 256)
    latent = ctx is not None

    h = _norm_mod(x, W["norm1"], mod, 0, 1, ts=ts)
    hf = h.reshape(ntok, D)
    uA = _mm(hf, W["wA"], tm=tm, tn=WA, name="proj_a").reshape(B, S, WA)
    uB = _mm(hf, W["wB"], tm=tm, tn=768, name="proj_b").reshape(B, S, WB)
    uC = _mm(hf, W["wC"], tm=tm, tn=768, name="proj_c").reshape(B, S, WC)
    uD = _mm(hf, W["wD"], tm=tm, tn=768, name="proj_d").reshape(B, S, WD)

    if latent:
        tabs = _rope_tables(S, MLA_NOPE, MLA_ROPE)
    else:
        tabs = tuple(jnp.zeros((S, LANES), F32) for _ in range(3))
    tsp = _tile(S, 512)
    q_mla = _mla_q(uA, W["q_a_norm"], W["wq"], W["q_norm"], tabs, ts=tsp, rope=latent)
    k_mla, v_mla, c_kv = _mla_kv(uA, MLA_Q_RANK // LANES, uA, MLA_Q_RANK // LANES + 1, W["kv_a_norm"],
                                 W["wk"], W["wv"], W["k_norm"], tabs, ts=tsp, rope=latent, normalise=True)
    if latent:
        C = ctx["mla_ckv"].shape[1]
        ctabs = tuple(jnp.zeros((C, LANES), F32) for _ in range(3))
        kr_c = jnp.pad(ctx["mla_krope"], ((0, 0), (0, 0), (MLA_NOPE, LANES - MLA_NOPE - MLA_ROPE)))
        k_c, v_c, _ = _mla_kv(ctx["mla_ckv"], 0, kr_c, 0, W["kv_a_norm"], W["wk"], W["wv"], W["k_norm"],
                              ctabs, ts=C, rope=False, normalise=False)
        k_mla = jnp.concatenate([k_mla, k_c], 2)
        v_mla = jnp.concatenate([v_mla, v_c], 2)
    y_a = _attn(q_mla, k_mla, v_mla, W["sink"], tq=tsq, use_sink=False)

    L = ML_CHUNK
    nc = S // L
    g0 = 4 * 512
    g_rows = uB[:, :, g0:g0 + 2 * LANES].reshape(B, nc, L, 2, LANES)[..., :2 * ML_HEADS]
    g_rows = jnp.transpose(g_rows, (0, 3, 1, 4, 2))
    if latent:
        C0 = jnp.pad(ctx["mlstm_C"], ((0, 0), (0, 0), (0, 0), (0, LANES - ML_DK), (0, 0)))
        n0 = _pad_cols(ctx["mlstm_n"], LANES)[:, :, :, None, :]
        m0 = jnp.broadcast_to(ctx["mlstm_m"][..., None, None], (B, 2, ML_HEADS, 1, LANES))
    else:
        C0 = jnp.zeros((B, 2, ML_HEADS, LANES, LANES), F32)
        n0 = jnp.zeros((B, 2, ML_HEADS, 1, LANES), F32)
        m0 = jnp.zeros((B, 2, ML_HEADS, 1, LANES), F32)
    h_fwd, h_bwd, C_f, n_f, m_f = _mlstm(uB, g_rows, W["ml_bias_col"], W["ml_bias_row"], C0, n0, m0)
    y_b = _mlstm_post(h_fwd, h_bwd, uB, W["ml_norm"], ts=_tile(S, 512))

    if latent:
        stabs = _rope_tables(S, 0, SW_HD)
    else:
        stabs = tuple(jnp.zeros((S, LANES), F32) for _ in range(3))
    sq, sk, sv, sk_raw = _swa_prep(uC, W["swa_q_norm"], W["swa_k_norm"], stabs, ts=tsq, rope=latent)
    if latent:
        kx = _pad_cols(jnp.transpose(ctx["swa_k"], (0, 2, 1, 3)), LANES).astype(BF16)
        vx = jnp.transpose(ctx["swa_v"], (0, 2, 1, 3))
        vx = _pad_cols(jnp.concatenate([vx, jnp.ones(vx.shape[:-1] + (1,), F32)], -1), LANES).astype(BF16)
        y_c = _win_attn(sq, sk, sv, kx, vx, W["sink_g"])
    else:
        y_c = _attn(sq, sk, sv, W["sink"], tq=tsq, use_sink=True)

    if latent:
        S0 = jnp.pad(ctx["rwkv"], ((0, 0), (0, 0), (0, 0), (0, LANES - RW_N), (0, LANES - RW_N)))
    else:
        S0 = jnp.zeros((B, 2, RW_HEADS, LANES, LANES), F32)
    y_fwd, y_bwd, S_rw = _rwkv(uD, W["rw_w2"], W["rw_w0"], W["rw_a2"], W["rw_a0"], W["rw_kk"], W["rw_ka"], S0)
    y_d = _rwkv_post(y_fwd, y_bwd, uD, W["rw_a2"], W["rw_a0"], W["rw_ka"], W["rw_u"], W["rw_g2"],
                     W["rw_gn_g"], W["rw_gn_b"], ts=_tile(S, 256))

    tok = lambda y: y.reshape(Bm, Sm, y.shape[-1])
    x = _merge(h, tok(y_a), tok(y_b), tok(y_c), tok(y_d), x, mod, W["wG"], W["wO"], W["w_out"], ts=_tile(Sm, 1024))
    h2 = _norm_mod(x, W["norm2"], mod, 3, 4, ts=ts)
    x = _mlp(h2, x, mod, W["mlp_w1"], W["mlp_w2"], ts=ts, tf=512)

    new_ctx = None
    if not latent:
        kr0 = MLA_Q_RANK + MLA_KV_RANK + MLA_NOPE
        v0 = (SW_HEADS + SW_KV_HEADS) * LANES
        new_ctx = {
            "mla_ckv": c_kv,
            "mla_krope": uA[:, :, kr0:kr0 + MLA_ROPE],
            "swa_k": sk_raw.reshape(B, S, SW_KV_HEADS, LANES)[..., :SW_HD],
            "swa_v": uC[:, :, v0:].reshape(B, S, SW_KV_HEADS, LANES)[..., :SW_HD],
            "mlstm_C": C_f[:, :, :, :ML_DK, :],
            "mlstm_n": n_f[:, :, :, 0, :ML_DK],
            "mlstm_m": m_f[:, :, :, 0, 0],
            "rwkv": S_rw[:, :, :, :RW_N, :RW_N],
        }
    return x, new_ctx


def kernel(x_prompt, x_sample, c, cache_mla_ckv, cache_mla_krope, cache_swa_k, cache_swa_v, state_mlstm_C, state_mlstm_n, state_mlstm_m, state_rwkv, c_ctx, ada_w, ada_b, norm1, norm2, w_in, mla_q_a_norm, mla_kv_a_norm, mla_w_uq, mla_w_ukv, mla_q_norm, mla_k_norm, mla_w_o, mlstm_i_bias, mlstm_f_bias, mlstm_norm, mlstm_w_o, swa_q_norm, swa_k_norm, swa_sink, swa_w_o, rwkv_w0, rwkv_w2, rwkv_a0, rwkv_a2, rwkv_g2, rwkv_kk, rwkv_ka, rwkv_u, rwkv_gn_g, rwkv_gn_b, rwkv_w_o, w_out, mlp_w1, mlp_w2):
    P = {'norm1': norm1, 'norm2': norm2, 'w_in': w_in,
         'mla_q_a_norm': mla_q_a_norm, 'mla_kv_a_norm': mla_kv_a_norm, 'mla_w_uq': mla_w_uq,
         'mla_w_ukv': mla_w_ukv, 'mla_q_norm': mla_q_norm, 'mla_k_norm': mla_k_norm, 'mla_w_o': mla_w_o,
         'mlstm_i_bias': mlstm_i_bias, 'mlstm_f_bias': mlstm_f_bias, 'mlstm_norm': mlstm_norm, 'mlstm_w_o': mlstm_w_o,
         'swa_q_norm': swa_q_norm, 'swa_k_norm': swa_k_norm, 'swa_sink': swa_sink, 'swa_w_o': swa_w_o,
         'rwkv_w0': rwkv_w0, 'rwkv_w2': rwkv_w2, 'rwkv_a0': rwkv_a0, 'rwkv_a2': rwkv_a2, 'rwkv_g2': rwkv_g2,
         'rwkv_kk': rwkv_kk, 'rwkv_ka': rwkv_ka, 'rwkv_u': rwkv_u, 'rwkv_gn_g': rwkv_gn_g, 'rwkv_gn_b': rwkv_gn_b,
         'rwkv_w_o': rwkv_w_o, 'w_out': w_out, 'mlp_w1': mlp_w1, 'mlp_w2': mlp_w2}
    depth = w_in.shape[0]
    Bc, Sc, D = x_prompt.shape
    Bl, Sl, _ = x_sample.shape

    cond = jnp.concatenate([c, c_ctx[None, :]], 0)
    rows = cond.shape[0]
    rows_pad = -(-rows // 16) * 16
    cond = jnp.pad(cond, ((0, rows_pad - rows), (0, 0)))

    xc = x_prompt.reshape(1, Bc * Sc, D)
    xl = x_sample
    states = []
    W_all = jax.vmap(_layer_weights)(P)
    for l in range(depth):
        W = jax.tree.map(lambda a: a[l], W_all)
        mod = _mm(cond, ada_w[l].astype(BF16), ada_b[l][None, :], tm=rows_pad, tn=1024, pre="silu", name="ada")
        mod_l = mod[:Bl, None, :]
        mod_c = mod[Bl:Bl + 1, None, :]
        xc, st = _trunk_layer(xc, mod_c, W, None, Bc, Sc)
        states.append(st)
        ctx = {'mla_ckv': cache_mla_ckv[:, l], 'mla_krope': cache_mla_krope[:, l],
               'swa_k': cache_swa_k[:, l], 'swa_v': cache_swa_v[:, l],
               'mlstm_C': state_mlstm_C[:, l], 'mlstm_n': state_mlstm_n[:, l], 'mlstm_m': state_mlstm_m[:, l],
               'rwkv': state_rwkv[:, l]}
        xl, _ = _trunk_layer(xl, mod_l, W, ctx, Bl, Sl)

    y_prompt = xc.reshape(Bc, Sc, D)
    stack = lambda name: jnp.stack([s[name] for s in states], 1)
    return (y_prompt, xl, stack('mla_ckv'), stack('mla_krope'), stack('swa_k'), stack('swa_v'),
            stack('mlstm_C'), stack('mlstm_n'), stack('mlstm_m'), stack('rwkv'))
```
